```python
import math
import jax
import jax.numpy as jnp
from jax import lax
import numpy as np

D_MODEL = 1024
BATCH = 4
SEQ = 8192
DEPTH = 2
DEC_BATCH = 32
DEC_SEQ = 16
PAST_LEN = 4096

CHUNK = 64
N_BACK = 8
ATT_HEADS = 8
ATT_HEAD_DIM = 64
ATT_W = ATT_HEADS * ATT_HEAD_DIM
REL_CLIP = 128
CONV_CH = 512
CONV_WIDTH = 31
DN_HEADS = 4
DN_DK = 128
DN_DV = 128
DN_QK_W = DN_HEADS * DN_DK
DN_W = DN_HEADS * DN_DV
DN_CONV_CH = 2 * DN_QK_W + DN_W
SHORT_CONV = 4
N_BRANCH = 3
BRANCH_W = 512
D_FF = 2816
N_EXPERTS = 8
TOP_K = 2
D_FF_EXPERT = 3584
N_DENSE = (DEPTH + 1) // 2
N_MOE = DEPTH // 2
NORM_EPS = 1e-6

OFF_AQ = 0
OFF_AK = OFF_AQ + ATT_W
OFF_AV = OFF_AK + ATT_W
OFF_GLU = OFF_AV + ATT_W
OFF_DN_QKV = OFF_GLU + 2 * CONV_CH
OFF_DN_Z = OFF_DN_QKV + DN_CONV_CH
OFF_DN_A = OFF_DN_Z + DN_W
OFF_DN_B = OFF_DN_A + DN_HEADS
OFF_GATE = OFF_DN_B + DN_HEADS
N_IN = OFF_GATE + N_BRANCH * D_MODEL

kernel_name = 'chunk_stream_hybrid_step'


def rmsnorm(x, g):
    xf = x.astype(jnp.float32)
    y = xf * lax.rsqrt(jnp.mean(xf * xf, axis=-1, keepdims=True) + NORM_EPS)
    return (y * g.astype(jnp.float32)).astype(x.dtype)


def layernorm(x, g, b):
    xf = x.astype(jnp.float32)
    mu = jnp.mean(xf, axis=-1, keepdims=True)
    var = jnp.mean(jnp.square(xf - mu), axis=-1, keepdims=True)
    y = (xf - mu) * lax.rsqrt(var + NORM_EPS) * g.astype(jnp.float32) + b.astype(jnp.float32)
    return y.astype(x.dtype)


def l2norm(x):
    return x * lax.rsqrt(jnp.sum(x * x, axis=-1, keepdims=True) + 1e-6)


def causal_dwconv(x, buf, w):
    xp = jnp.concatenate([buf.astype(x.dtype), x], axis=1)
    y = lax.conv_general_dilated(xp, w.astype(x.dtype)[:, None, :], window_strides=(1,), padding='VALID',
                                 dimension_numbers=('NWC', 'WIO', 'NWC'), feature_group_count=x.shape[-1])
    return y, xp[:, xp.shape[1] - (w.shape[0] - 1):]


def chunk_band(t, n_chunks):
    n, _, h, d = t.shape
    tp = jnp.concatenate([jnp.zeros((n, N_BACK * CHUNK, h, d), t.dtype), t], axis=1)
    tp = tp.reshape(n, n_chunks + N_BACK, CHUNK, h, d)
    return jnp.concatenate([tp[:, o:o + n_chunks] for o in range(N_BACK + 1)], axis=2)


def band_attention(q, k, v, qpos, kpos, rel_bias):
    s = jnp.einsum('nglhd,ngmhd->nghlm', q, k).astype(jnp.float32) * (q.shape[-1] ** -0.5)
    rel = jnp.clip(qpos[:, :, None] - kpos[:, None, :], -REL_CLIP, REL_CLIP) + REL_CLIP
    bias = jnp.moveaxis(rel_bias.astype(jnp.float32)[:, rel], 0, 1)
    qc = (qpos // CHUNK)[:, :, None]
    kc = (kpos // CHUNK)[:, None, :]
    ok = (kpos[:, None, :] >= 0) & (kc <= qc) & (kc >= qc - N_BACK)
    s = jnp.where(ok[:, None], s + bias, -jnp.inf)
    p = jax.nn.softmax(s, axis=-1).astype(v.dtype)
    return jnp.einsum('nghlm,ngmhd->nglhd', p, v)


def gated_delta_chunked(q, k, v, g, beta, S0, cs):
    n, L, h, _ = q.shape
    nc = L // cs

    def blocks(t):
        t = t.reshape((n, nc, cs) + t.shape[2:])
        return jnp.moveaxis(jnp.moveaxis(t, 1, 0), 2, 3)

    qb, kb, vb, gb, bb = blocks(q), blocks(k), blocks(v), blocks(g), blocks(beta)
    gcum = jnp.cumsum(gb, axis=-1)
    incl = jnp.tril(jnp.ones((cs, cs), bool))
    strict = jnp.tril(jnp.ones((cs, cs), bool), -1)
    decay = jnp.exp(jnp.where(incl, gcum[..., :, None] - gcum[..., None, :], -jnp.inf))
    k_beta = kb * bb[..., None]
    A = jnp.where(strict, jnp.einsum('cnhid,cnhjd->cnhij', k_beta, kb) * decay, 0.0)
    eye = jnp.eye(cs, dtype=jnp.float32)
    T = lax.linalg.triangular_solve(eye + A, jnp.broadcast_to(eye, A.shape), left_side=True,
                                    lower=True, unit_diagonal=True)
    w = T @ (k_beta * jnp.exp(gcum)[..., None])
    u = T @ (vb * bb[..., None])
    qk = jnp.einsum('cnhid,cnhjd->cnhij', qb, kb) * decay
    qg = qb * jnp.exp(gcum)[..., None]
    kd = kb * jnp.exp(gcum[..., -1:] - gcum)[..., None]
    glast = jnp.exp(gcum[..., -1])

    def step(S, xs):
        w_i, u_i, qg_i, qk_i, kd_i, gl_i = xs
        v_new = u_i - jnp.einsum('nhcd,nhde->nhce', w_i, S)
        o_i = jnp.einsum('nhcd,nhde->nhce', qg_i, S) + jnp.einsum('nhij,nhje->nhie', qk_i, v_new)
        S = S * gl_i[..., None, None] + jnp.einsum('nhcd,nhce->nhde', kd_i, v_new)
        return S, o_i

    S, o = lax.scan(step, S0, (w, u, qg, qk, kd, glast))
    o = jnp.moveaxis(jnp.moveaxis(o, 3, 2), 0, 1).reshape(n, L, h, o.shape[-1])
    return o, S


def token_mixers(h, pos0, att_cache, conv_buf, dn_buf, dn_S, w_in_l, rel_bias_l, conv_w_l, conv_b_l,
                 conv_ln_g_l, conv_ln_b_l, dn_conv_w_l, dn_A_log_l, dn_dt_bias_l, dn_norm_g_l,
                 w_branch_l, w_out_l):
    n, L, _ = h.shape
    z = h @ w_in_l

    q = z[..., OFF_AQ:OFF_AK].reshape(n, L, ATT_HEADS, ATT_HEAD_DIM)
    k = z[..., OFF_AK:OFF_AV].reshape(n, L, ATT_HEADS, ATT_HEAD_DIM)
    v = z[..., OFF_AV:OFF_GLU].reshape(n, L, ATT_HEADS, ATT_HEAD_DIM)
    if att_cache is None:
        nc = L // CHUNK
        qg = q.reshape(n, nc, CHUNK, ATT_HEADS, ATT_HEAD_DIM)
        kb, vb = chunk_band(k, nc), chunk_band(v, nc)
        cidx = jnp.arange(nc)[:, None]
        qpos = cidx * CHUNK + jnp.arange(CHUNK)[None]
        kpos = (cidx - N_BACK) * CHUNK + jnp.arange((N_BACK + 1) * CHUNK)[None]
        keep = min(N_BACK * CHUNK, L)
        k_rows, v_rows = k[:, L - keep:], v[:, L - keep:]
    else:
        cache_k, cache_v = att_cache
        nr = cache_k.shape[1]
        qg = q[:, None]
        kb = jnp.concatenate([cache_k.astype(k.dtype), k], axis=1)[:, None]
        vb = jnp.concatenate([cache_v.astype(v.dtype), v], axis=1)[:, None]
        qpos = (pos0 + jnp.arange(L))[None]
        kpos = (pos0 - nr + jnp.arange(nr + L))[None]
        k_rows, v_rows = k, v
    att_out = band_attention(qg, kb, vb, qpos, kpos, rel_bias_l).reshape(n, L, ATT_W)

    u = z[..., OFF_GLU:OFF_DN_QKV]
    glu = u[..., :CONV_CH] * jax.nn.sigmoid(u[..., CONV_CH:])
    cv, new_conv_buf = causal_dwconv(glu, conv_buf, conv_w_l)
    conv_out = jax.nn.silu(layernorm(cv + conv_b_l, conv_ln_g_l, conv_ln_b_l))

    qkv, new_dn_buf = causal_dwconv(z[..., OFF_DN_QKV:OFF_DN_Z], dn_buf, dn_conv_w_l)
    qkv = jax.nn.silu(qkv).astype(jnp.float32)
    dq = l2norm(qkv[..., :DN_QK_W].reshape(n, L, DN_HEADS, DN_DK)) * (DN_DK ** -0.5)
    dk = l2norm(qkv[..., DN_QK_W:2 * DN_QK_W].reshape(n, L, DN_HEADS, DN_DK))
    dv = qkv[..., 2 * DN_QK_W:].reshape(n, L, DN_HEADS, DN_DV)
    da = z[..., OFF_DN_A:OFF_DN_B].astype(jnp.float32)
    db = z[..., OFF_DN_B:OFF_GATE].astype(jnp.float32)
    beta = jax.nn.sigmoid(db)
    g = -jnp.exp(dn_A_log_l.astype(jnp.float32)) * jax.nn.softplus(da + dn_dt_bias_l.astype(jnp.float32))
    cs = L if L <= CHUNK else CHUNK
    o, S_new = gated_delta_chunked(dq, dk, dv, g, beta, dn_S.astype(jnp.float32), cs)
    dz = z[..., OFF_DN_Z:OFF_DN_A].astype(jnp.float32).reshape(n, L, DN_HEADS, DN_DV)
    dn_out = (rmsnorm(o, dn_norm_g_l) * jax.nn.silu(dz)).reshape(n, L, DN_W).astype(h.dtype)

    branches = jnp.stack([att_out, conv_out, dn_out], axis=2)
    y_b = jnp.einsum('nlbw,bwd->nlbd', branches, w_branch_l)
    gates = jax.nn.sigmoid(z[..., OFF_GATE:].reshape(n, L, N_BRANCH, D_MODEL))
    out = jnp.sum(gates * y_b, axis=2) @ w_out_l
    return out, (k_rows, v_rows, new_conv_buf, new_dn_buf, S_new.astype(h.dtype))


def swiglu(h, w_i, w_o):
    a, b = jnp.split(h @ w_i, 2, axis=-1)
    return (jax.nn.silu(a) * b) @ w_o


def moe_swiglu(h, w_r, w_i, w_o):
    logits = (h @ w_r).astype(jnp.float32)
    top_v, top_i = lax.top_k(logits, TOP_K)
    wts = jax.nn.softmax(top_v, axis=-1)
    gate = jnp.sum(jax.nn.one_hot(top_i, N_EXPERTS, dtype=jnp.float32) * wts[..., None], axis=-2).astype(h.dtype)
    out = jnp.zeros(h.shape[:-1] + (w_o.shape[-1],), h.dtype)
    for e in range(N_EXPERTS):
        out = out + gate[..., e:e + 1] * swiglu(h, w_i[e], w_o[e])
    return out


def trunk(x, c, pos0, caches, w):
    (w_ada, b_ada, norm_g, w_in, rel_bias, conv_w, conv_b, conv_ln_g, conv_ln_b, dn_conv_w, dn_A_log,
     dn_dt_bias, dn_norm_g, w_branch, w_out, ffn_w_in, ffn_w_out, router_w, moe_w_in, moe_w_out,
     final_norm_g) = w
    n = x.shape[0]
    new_states = ([], [], [], [], [])
    for l in range(DEPTH):
        mod = (jax.nn.silu(c) @ w_ada[l] + b_ada[l]).reshape(n, 6, 1, D_MODEL)
        shift_m, scale_m, gate_m, shift_f, scale_f, gate_f = (mod[:, i] for i in range(6))
        if caches is None:
            att_cache = None
            conv_buf = jnp.zeros((n, CONV_WIDTH - 1, CONV_CH), x.dtype)
            dn_buf = jnp.zeros((n, SHORT_CONV - 1, DN_CONV_CH), x.dtype)
            dn_S = jnp.zeros((n, DN_HEADS, DN_DK, DN_DV), jnp.float32)
        else:
            cache_k, cache_v, cache_conv, cache_dn_conv, cache_dn = caches
            att_cache = (cache_k[l], cache_v[l])
            conv_buf = cache_conv[l]
            dn_buf = cache_dn_conv[l]
            dn_S = cache_dn[l]
        h = rmsnorm(x, norm_g[l, 0]) * (1 + scale_m) + shift_m
        mix, states = token_mixers(h, pos0, att_cache, conv_buf, dn_buf, dn_S, w_in[l], rel_bias[l],
                                   conv_w[l], conv_b[l], conv_ln_g[l], conv_ln_b[l], dn_conv_w[l],
                                   dn_A_log[l], dn_dt_bias[l], dn_norm_g[l], w_branch[l], w_out[l])
        for lst, s in zip(new_states, states):
            lst.append(s)
        x = x + gate_m * mix
        h = rmsnorm(x, norm_g[l, 1]) * (1 + scale_f) + shift_f
        if l % 2 == 0:
            f = swiglu(h, ffn_w_in[l // 2], ffn_w_out[l // 2])
        else:
            f = moe_swiglu(h, router_w[l // 2], moe_w_in[l // 2], moe_w_out[l // 2])
        x = x + gate_f * f
    return rmsnorm(x, final_norm_g), tuple(jnp.stack(s) for s in new_states)


def setup_inputs(seed: int = 0) -> dict:
    key = jax.random.key(seed)
    ks = iter(jax.random.split(key, 40))

    def nrm(shape, scale):
        return jax.random.normal(next(ks), shape, jnp.float32) * scale

    a_rows = min(N_BACK * CHUNK, PAST_LEN)
    dt = jnp.exp(jax.random.uniform(next(ks), (DEPTH, DN_HEADS), jnp.float32, math.log(1e-3), math.log(1e-1)))
    return {
        'x_prompt': nrm((BATCH, SEQ, D_MODEL), 1.0),
        'x_sample': nrm((DEC_BATCH, DEC_SEQ, D_MODEL), 1.0),
        'c_prompt': nrm((BATCH, D_MODEL), 1.0),
        'c_sample': nrm((DEC_BATCH, D_MODEL), 1.0),
        'cache_attn_k': nrm((DEPTH, DEC_BATCH, a_rows, ATT_HEADS, ATT_HEAD_DIM), 1.0),
        'cache_attn_v': nrm((DEPTH, DEC_BATCH, a_rows, ATT_HEADS, ATT_HEAD_DIM), 1.0),
        'state_conv': nrm((DEPTH, DEC_BATCH, CONV_WIDTH - 1, CONV_CH), 0.5),
        'state_dn_conv': nrm((DEPTH, DEC_BATCH, SHORT_CONV - 1, DN_CONV_CH), 1.0),
        'state_dn': nrm((DEPTH, DEC_BATCH, DN_HEADS, DN_DK, DN_DV), 0.1),
        'w_ada': nrm((DEPTH, D_MODEL, 6 * D_MODEL), D_MODEL ** -0.5),
        'b_ada': nrm((DEPTH, 6 * D_MODEL), 0.01),
        'norm_g': 1.0 + nrm((DEPTH, 2, D_MODEL), 0.01),
        'w_in': nrm((DEPTH, D_MODEL, N_IN), D_MODEL ** -0.5),
        'rel_bias': nrm((DEPTH, ATT_HEADS, 2 * REL_CLIP + 1), 0.5),
        'conv_w': nrm((DEPTH, CONV_WIDTH, CONV_CH), CONV_WIDTH ** -0.5),
        'conv_b': nrm((DEPTH, CONV_CH), 0.01),
        'conv_ln_g': 1.0 + nrm((DEPTH, CONV_CH), 0.01),
        'conv_ln_b': nrm((DEPTH, CONV_CH), 0.01),
        'dn_conv_w': nrm((DEPTH, SHORT_CONV, DN_CONV_CH), SHORT_CONV ** -0.5),
        'dn_A_log': jnp.log(jax.random.uniform(next(ks), (DEPTH, DN_HEADS), jnp.float32, 1.0, 16.0)),
        'dn_dt_bias': dt + jnp.log(-jnp.expm1(-dt)),
        'dn_norm_g': 1.0 + nrm((DEPTH, DN_DV), 0.01),
        'w_branch': nrm((DEPTH, N_BRANCH, BRANCH_W, D_MODEL), BRANCH_W ** -0.5),
        'w_out': nrm((DEPTH, D_MODEL, D_MODEL), D_MODEL ** -0.5),
        'ffn_w_in': nrm((N_DENSE, D_MODEL, 2 * D_FF), D_MODEL ** -0.5),
        'ffn_w_out': nrm((N_DENSE, D_FF, D_MODEL), D_FF ** -0.5),
        'router_w': nrm((N_MOE, D_MODEL, N_EXPERTS), D_MODEL ** -0.5),
        'moe_w_in': nrm((N_MOE, N_EXPERTS, D_MODEL, 2 * D_FF_EXPERT), D_MODEL ** -0.5),
        'moe_w_out': nrm((N_MOE, N_EXPERTS, D_FF_EXPERT, D_MODEL), D_FF_EXPERT ** -0.5),
        'final_norm_g': 1.0 + nrm((D_MODEL,), 0.01),
    }


def reference(x_prompt, x_sample, c_prompt, c_sample, cache_attn_k, cache_attn_v, state_conv,
              state_dn_conv, state_dn, w_ada, b_ada, norm_g, w_in, rel_bias, conv_w, conv_b, conv_ln_g,
              conv_ln_b, dn_conv_w, dn_A_log, dn_dt_bias, dn_norm_g, w_branch, w_out, ffn_w_in,
              ffn_w_out, router_w, moe_w_in, moe_w_out, final_norm_g):
    weights = (w_ada, b_ada, norm_g, w_in, rel_bias, conv_w, conv_b, conv_ln_g, conv_ln_b, dn_conv_w,
               dn_A_log, dn_dt_bias, dn_norm_g, w_branch, w_out, ffn_w_in, ffn_w_out, router_w,
               moe_w_in, moe_w_out, final_norm_g)
    y_prompt, (k_p, v_p, conv_p, dn_conv_p, dn_p) = trunk(x_prompt, c_prompt, 0, None, weights)
    y_sample, (k_s, v_s, conv_s, dn_conv_s, dn_s) = trunk(
        x_sample, c_sample, PAST_LEN,
        (cache_attn_k, cache_attn_v, state_conv, state_dn_conv, state_dn), weights)
    return (y_prompt, y_sample, k_p, v_p, conv_p, dn_conv_p, dn_p, k_s, v_s, conv_s, dn_conv_s, dn_s)
```

```python
import functools

import jax
import jax.numpy as jnp
from jax import lax
from jax.experimental import pallas as pl
from jax.experimental.pallas import tpu as pltpu

F32 = jnp.float32
BF16 = jnp.bfloat16

PAST_LEN = 4096
CHUNK = 64
N_BACK = 8
ATT_HEADS = 8
ATT_HEAD_DIM = 64
ATT_W = ATT_HEADS * ATT_HEAD_DIM
REL_CLIP = 128
CONV_CH = 512
CONV_WIDTH = 31
DN_HEADS = 4
DN_DK = 128
DN_DV = 128
DN_QK_W = DN_HEADS * DN_DK
DN_W = DN_HEADS * DN_DV
DN_CONV_CH = 2 * DN_QK_W + DN_W
SHORT_CONV = 4
N_BRANCH = 3
BRANCH_W = 512
TOP_K = 2
NORM_EPS = 1e-6
NEG_BIG = -1e30

OFF_AQ = 0
OFF_GLU = 3 * ATT_W
OFF_DN_QKV = OFF_GLU + 2 * CONV_CH
OFF_DN_Z = OFF_DN_QKV + DN_CONV_CH
OFF_DN_A = OFF_DN_Z + DN_W
OFF_GATE = OFF_DN_A + 2 * DN_HEADS

Z_GATE = 0
Z_DNQKV = 3072
Z_Q = 4608
Z_K = 5120
Z_V = 5632
Z_U1 = 6144
Z_U2 = 6656
Z_DZ = 7168
Z_W = 7680
LANES = 128
HALO = 32
DN_HALO = 8

VMEM_LIMIT = 56 * 1024 * 1024


def _cparams(sem):
    return pltpu.CompilerParams(dimension_semantics=sem, vmem_limit_bytes=VMEM_LIMIT)


def _sigmoid(x):
    return 1.0 / (1.0 + jnp.exp(-x))


def _silu(x):
    return x * _sigmoid(x)


def _mod_norm(x, g, scale, shift):
    ms = jnp.mean(x * x, axis=-1, keepdims=True)
    return (x * lax.rsqrt(ms + NORM_EPS) * g) * (1.0 + scale) + shift


def _ada_kernel(c_ref, w_ref, b_ref, o_ref):
    s = _silu(c_ref[...])
    o_ref[0] = jnp.dot(s.astype(BF16), w_ref[0].astype(BF16), preferred_element_type=F32) + b_ref[0]


def _ada(c_all, w_ada, b_ada):
    depth, d, n6 = w_ada.shape
    n = c_all.shape[0]
    tn = 1024
    return pl.pallas_call(
        _ada_kernel,
        grid=(depth, n6 // tn),
        in_specs=[pl.BlockSpec((n, d), lambda l, j: (0, 0)),
                  pl.BlockSpec((1, d, tn), lambda l, j: (l, 0, j)),
                  pl.BlockSpec((1, 1, tn), lambda l, j: (l, 0, j))],
        out_specs=pl.BlockSpec((1, n, tn), lambda l, j: (l, 0, j)),
        out_shape=jax.ShapeDtypeStruct((depth, n, n6), F32),
        compiler_params=_cparams(("parallel", "parallel")),
        name="ada_mod",
    )(c_all, w_ada, b_ada.reshape(depth, 1, n6))


def _in_proj_kernel(x_ref, g_ref, sc_ref, sh_ref, w_ref, wab_ref, z_ref, zab_ref, h_scr):
    @pl.when(pl.program_id(1) == 0)
    def _():
        h = _mod_norm(x_ref[...], g_ref[...], sc_ref[0], sh_ref[0]).astype(BF16)
        h_scr[...] = h
        zab_ref[...] = jnp.dot(h, wab_ref[...], preferred_element_type=F32)

    z_ref[...] = jnp.dot(h_scr[...], w_ref[...], preferred_element_type=F32).astype(BF16)


def _mod_spec(mod, tm, tps):
    r = mod.shape[1]
    if r == 1:
        return pl.BlockSpec((1, 1, mod.shape[2]), lambda i, *_: (i // tps, 0, 0))
    return pl.BlockSpec((1, r, mod.shape[2]), lambda i, *_: (i, 0, 0))


def _in_proj(x, g, scale, shift, w_main, w_ab, tm, tps):
    m, d = x.shape
    tn = 512
    nt = Z_W // tn
    return pl.pallas_call(
        _in_proj_kernel,
        grid=(m // tm, nt),
        in_specs=[pl.BlockSpec((tm, d), lambda i, j: (i, 0)),
                  pl.BlockSpec((1, d), lambda i, j: (0, 0)),
                  _mod_spec(scale, tm, tps), _mod_spec(shift, tm, tps),
                  pl.BlockSpec((d, tn), lambda i, j: (0, j)),
                  pl.BlockSpec((d, LANES), lambda i, j: (0, 0))],
        out_specs=[pl.BlockSpec((tm, tn), lambda i, j: (i, j)),
                   pl.BlockSpec((tm, LANES), lambda i, j: (i, 0))],
        out_shape=[jax.ShapeDtypeStruct((m, Z_W), BF16), jax.ShapeDtypeStruct((m, LANES), F32)],
        scratch_shapes=[pltpu.VMEM((tm, d), BF16)],
        compiler_params=_cparams(("parallel", "arbitrary")),
        name="in_proj",
    )(x, g, scale, shift, w_main, w_ab)


def _softmax_pv(parts, vs):
    m = parts[0].max(axis=-1, keepdims=True)
    for s in parts[1:]:
        m = jnp.maximum(m, s.max(axis=-1, keepdims=True))
    num = None
    den = None
    for s, v in zip(parts, vs):
        p = jnp.exp(s - m)
        l = p.sum(axis=-1, keepdims=True)
        o = jnp.dot(p.astype(BF16), v, preferred_element_type=F32)
        num = o if num is None else num + o
        den = l if den is None else den + l
    return num / den


def _qk(q, k):
    return lax.dot_general(q, k, (((1,), (1,)), ((), ())), preferred_element_type=F32)


def _attn_prompt_kernel(q_ref, kp_ref, kc_ref, vp_ref, vc_ref, tab_ref, o_ref, *, tps, tq, qb, win):
    first = (pl.program_id(0) % tps == 0).astype(F32)
    kwin = jnp.concatenate([kp_ref[...], kc_ref[...]], axis=0)
    vwin = jnp.concatenate([vp_ref[...], vc_ref[...]], axis=0)
    rowid = lax.broadcasted_iota(jnp.int32, (1, 2 * tq), 1)
    neg = jnp.where(rowid < tq, first * NEG_BIG, 0.0)
    lo = lax.broadcasted_iota(jnp.int32, (qb, LANES), 1) < ATT_HEAD_DIM
    scale = ATT_HEAD_DIM ** -0.5
    for b in range(tq // qb):
        negw = neg[:, b * qb:b * qb + win]
        for hp in range(ATT_HEADS // 2):
            cols = slice(hp * LANES, (hp + 1) * LANES)
            q = q_ref[b * qb:(b + 1) * qb, cols]
            kw = kwin[b * qb:b * qb + win, cols]
            vw = vwin[b * qb:b * qb + win, cols]
            outs = []
            for half in range(2):
                qm = jnp.where(lo if half == 0 else jnp.logical_not(lo), q, jnp.zeros_like(q))
                s = _qk(qm, kw) * scale + tab_ref[2 * hp + half] + negw
                outs.append(_softmax_pv([s], [vw]))
            o_ref[b * qb:(b + 1) * qb, cols] = jnp.where(lo, outs[0], outs[1]).astype(BF16)


def _attn_prompt(zmain, tab, n_seq, seq_len):
    m = zmain.shape[0]
    tq = N_BACK * CHUNK
    qb = 2 * CHUNK
    win = tq + qb
    tps = seq_len // tq
    cq, ck, cv = Z_Q // ATT_W, Z_K // ATT_W, Z_V // ATT_W

    def prev(i):
        return jnp.where(i % tps == 0, i, i - 1)

    return pl.pallas_call(
        functools.partial(_attn_prompt_kernel, tps=tps, tq=tq, qb=qb, win=win),
        grid=(m // tq,),
        in_specs=[pl.BlockSpec((tq, ATT_W), lambda i: (i, cq)),
                  pl.BlockSpec((tq, ATT_W), lambda i: (prev(i), ck)),
                  pl.BlockSpec((tq, ATT_W), lambda i: (i, ck)),
                  pl.BlockSpec((tq, ATT_W), lambda i: (prev(i), cv)),
                  pl.BlockSpec((tq, ATT_W), lambda i: (i, cv)),
                  pl.BlockSpec((ATT_HEADS, qb, win), lambda i: (0, 0, 0))],
        out_specs=pl.BlockSpec((tq, ATT_W), lambda i: (i, 0)),
        out_shape=jax.ShapeDtypeStruct((m, ATT_W), BF16),
        compiler_params=_cparams(("parallel",)),
        name="attn_prompt",
    )(zmain, zmain, zmain, zmain, zmain, tab)


def _attn_decode_kernel(q_ref, kc_ref, kn_ref, vc_ref, vn_ref, tabc_ref, tabn_ref, o_ref):
    lq = q_ref.shape[0]
    kc = kc_ref[0].astype(BF16)
    vc = vc_ref[0].astype(BF16)
    lo = lax.broadcasted_iota(jnp.int32, (lq, LANES), 1) < ATT_HEAD_DIM
    scale = ATT_HEAD_DIM ** -0.5
    for hp in range(ATT_HEADS // 2):
        cols = slice(hp * LANES, (hp + 1) * LANES)
        q = q_ref[:, cols]
        outs = []
        for half in range(2):
            qm = jnp.where(lo if half == 0 else jnp.logical_not(lo), q, jnp.zeros_like(q))
            s1 = _qk(qm, kc[:, cols]) * scale + tabc_ref[2 * hp + half]
            s2 = _qk(qm, kn_ref[:, cols]) * scale + tabn_ref[2 * hp + half]
            outs.append(_softmax_pv([s1, s2], [vc[:, cols], vn_ref[:, cols]]))
        o_ref[:, cols] = jnp.where(lo, outs[0], outs[1]).astype(BF16)


def _attn_decode(zmain, cache_k, cache_v, tab_c, tab_n, n_seq, lq):
    m = zmain.shape[0]
    nr = cache_k.shape[1]
    cq, ck, cv = Z_Q // ATT_W, Z_K // ATT_W, Z_V // ATT_W
    return pl.pallas_call(
        _attn_decode_kernel,
        grid=(n_seq,),
        in_specs=[pl.BlockSpec((lq, ATT_W), lambda i: (i, cq)),
                  pl.BlockSpec((1, nr, ATT_W), lambda i: (i, 0, 0)),
                  pl.BlockSpec((lq, ATT_W), lambda i: (i, ck)),
                  pl.BlockSpec((1, nr, ATT_W), lambda i: (i, 0, 0)),
                  pl.BlockSpec((lq, ATT_W), lambda i: (i, cv)),
                  pl.BlockSpec((ATT_HEADS, lq, nr), lambda i: (0, 0, 0)),
                  pl.BlockSpec((ATT_HEADS, lq, lq), lambda i: (0, 0, 0))],
        out_specs=pl.BlockSpec((lq, ATT_W), lambda i: (i, 0)),
        out_shape=jax.ShapeDtypeStruct((m, ATT_W), BF16),
        compiler_params=_cparams(("parallel",)),
        name="attn_decode",
    )(zmain, cache_k, zmain, cache_v, zmain, tab_c, tab_n)


def _bias_table(rel_bias_l, qpos, kpos):
    rel = jnp.clip(qpos[:, None] - kpos[None, :], -REL_CLIP, REL_CLIP) + REL_CLIP
    qc = (qpos // CHUNK)[:, None]
    kc = (kpos // CHUNK)[None, :]
    ok = (kpos[None, :] >= 0) & (kc <= qc) & (kc >= qc - N_BACK)
    return jnp.where(ok[None], rel_bias_l.astype(F32)[:, rel], NEG_BIG)


def _conv_kernel(u1_ref, u2_ref, buf_ref, w_ref, cb_ref, lg_ref, lb_ref, o_ref, nb_ref, xp_scr, *, t_rows, sub, n_t):
    t = pl.program_id(1)

    @pl.when(t == 0)
    def _():
        xp_scr[0:HALO, :] = buf_ref[0]

    u1 = u1_ref[...].astype(F32)
    u2 = u2_ref[...].astype(F32)
    xp_scr[HALO:HALO + t_rows, :] = u1 * _sigmoid(u2)
    off = HALO - (CONV_WIDTH - 1)
    for r0 in range(0, t_rows, sub):
        acc = w_ref[0:1, :] * xp_scr[r0 + off:r0 + off + sub, :]
        for j in range(1, CONV_WIDTH):
            acc = acc + w_ref[j:j + 1, :] * xp_scr[r0 + off + j:r0 + off + j + sub, :]
        cv = acc + cb_ref[...]
        mu = jnp.mean(cv, axis=-1, keepdims=True)
        cen = cv - mu
        var = jnp.mean(cen * cen, axis=-1, keepdims=True)
        y = cen * lax.rsqrt(var + NORM_EPS) * lg_ref[...] + lb_ref[...]
        o_ref[r0:r0 + sub, :] = _silu(y).astype(BF16)

    @pl.when(t == n_t - 1)
    def _():
        nb_ref[0] = xp_scr[t_rows + off:t_rows + HALO, :]

    xp_scr[0:HALO, :] = xp_scr[t_rows:t_rows + HALO, :]


def _conv_module(zmain, buf, w, cb, lg, lb, n_seq, seq_len, t_rows, sub):
    m = zmain.shape[0]
    n_t = seq_len // t_rows
    c1, c2 = Z_U1 // CONV_CH, Z_U2 // CONV_CH
    vec = pl.BlockSpec((1, CONV_CH), lambda n, t: (0, 0))
    return pl.pallas_call(
        functools.partial(_conv_kernel, t_rows=t_rows, sub=sub, n_t=n_t),
        grid=(n_seq, n_t),
        in_specs=[pl.BlockSpec((t_rows, CONV_CH), lambda n, t: (n * n_t + t, c1)),
                  pl.BlockSpec((t_rows, CONV_CH), lambda n, t: (n * n_t + t, c2)),
                  pl.BlockSpec((1, HALO, CONV_CH), lambda n, t: (n, 0, 0)),
                  pl.BlockSpec((HALO, CONV_CH), lambda n, t: (0, 0)),
                  vec, vec, vec],
        out_specs=[pl.BlockSpec((t_rows, CONV_CH), lambda n, t: (n * n_t + t, 0)),
                   pl.BlockSpec((1, CONV_WIDTH - 1, CONV_CH), lambda n, t: (n, 0, 0))],
        out_shape=[jax.ShapeDtypeStruct((m, CONV_CH), BF16),
                   jax.ShapeDtypeStruct((n_seq, CONV_WIDTH - 1, CONV_CH), F32)],
        scratch_shapes=[pltpu.VMEM((HALO + t_rows, CONV_CH), F32)],
        compiler_params=_cparams(("parallel", "arbitrary")),
        name="conv_module",
    )(zmain, zmain, buf, w, cb, lg, lb)


def _unit_lower_inverse(a, ii, jj, cs):
    d = jnp.where(ii == jj, 1.0, 0.0)
    s = 1
    sh = 0
    while s < cs:
        bi = ii >> sh
        bj = jj >> sh
        sel = jnp.where((bi & 1) == 1, bi - 1, -1) == bj
        bm = jnp.where(sel, a, 0.0)
        if s == 1:
            d = d - bm
        else:
            x = jnp.dot(d, bm, preferred_element_type=F32, precision=lax.Precision.HIGHEST)
            d = d - jnp.dot(x, d, preferred_element_type=F32, precision=lax.Precision.HIGHEST)
        s *= 2
        sh += 1
    return d


def _dn_kernel(qkv_ref, dz_ref, ab_ref, buf_ref, s0_ref, cw_ref, expa_ref, dtb_ref, ng_ref,
               o_ref, nbuf_ref, sfin_ref, xp_scr, s_scr, *, nb, cs, n_c):
    c = pl.program_id(1)

    @pl.when(c == 0)
    def _():
        xp_scr[:, 0:DN_HALO, :] = buf_ref[...]
        s_scr[...] = s0_ref[...]

    ii = lax.broadcasted_iota(jnp.int32, (cs, cs), 0)
    jj = lax.broadcasted_iota(jnp.int32, (cs, cs), 1)
    incl = ii >= jj
    lincl = jnp.where(incl, 1.0, 0.0)
    uincl = jnp.where(ii <= jj, 1.0, 0.0)
    off = DN_HALO - (SHORT_CONV - 1)
    hi = lax.Precision.HIGHEST
    for b in range(nb):
        xp_scr[b, DN_HALO:DN_HALO + cs, :] = qkv_ref[b].astype(F32)
        y = cw_ref[0:1, :] * xp_scr[b, off:off + cs, :]
        for j in range(1, SHORT_CONV):
            y = y + cw_ref[j:j + 1, :] * xp_scr[b, off + j:off + j + cs, :]
        y = _silu(y)
        ab = ab_ref[b]
        xg = ab + dtb_ref[...]
        softplus = jnp.maximum(xg, 0.0) + jnp.log(1.0 + jnp.exp(-jnp.abs(xg)))
        gfull = -expa_ref[...] * softplus
        beta_full = _sigmoid(ab)
        gcum = jnp.dot(lincl, gfull, preferred_element_type=F32, precision=hi)
        gcum_t = lax.dot_general(gfull, uincl, (((0,), (0,)), ((), ())),
                                 preferred_element_type=F32, precision=hi)
        dz = dz_ref[b].astype(F32)
        for h in range(DN_HEADS):
            q = y[:, h * DN_DK:(h + 1) * DN_DK]
            k = y[:, DN_QK_W + h * DN_DK:DN_QK_W + (h + 1) * DN_DK]
            v = y[:, 2 * DN_QK_W + h * DN_DV:2 * DN_QK_W + (h + 1) * DN_DV]
            q = q * lax.rsqrt(jnp.sum(q * q, axis=-1, keepdims=True) + 1e-6) * (DN_DK ** -0.5)
            k = k * lax.rsqrt(jnp.sum(k * k, axis=-1, keepdims=True) + 1e-6)
            gc = gcum[:, h:h + 1]
            gr = gcum_t[h:h + 1, :]
            beta = beta_full[:, DN_HEADS + h:DN_HEADS + h + 1]
            glast = gcum[cs - 1:cs, h:h + 1]
            eg = jnp.exp(gc)
            decay = jnp.exp(jnp.where(incl, gc - gr, NEG_BIG))
            kb = k * beta
            kk = _qk(jnp.concatenate([kb, q], axis=0).astype(BF16), k.astype(BF16))
            a = jnp.where(ii > jj, kk[:cs] * decay, 0.0)
            qk = kk[cs:] * decay
            tinv = _unit_lower_inverse(a, ii, jj, cs)
            wu = jnp.dot(tinv.astype(BF16), jnp.concatenate([kb * eg, v * beta], axis=1).astype(BF16),
                         preferred_element_type=F32)
            w = wu[:, :DN_DK]
            u = wu[:, DN_DK:]
            qg = q * eg
            kd = k * jnp.exp(glast - gc)
            s_old = s_scr[b, h]
            wq = jnp.dot(jnp.concatenate([w, qg], axis=0).astype(BF16), s_old.astype(BF16),
                         preferred_element_type=F32)
            v_new = u - wq[:cs]
            o = wq[cs:] + jnp.dot(qk.astype(BF16), v_new.astype(BF16), preferred_element_type=F32)
            s_scr[b, h] = s_old * jnp.exp(glast) + lax.dot_general(
                kd.astype(BF16), v_new.astype(BF16), (((0,), (0,)), ((), ())), preferred_element_type=F32)
            on = o * lax.rsqrt(jnp.mean(o * o, axis=-1, keepdims=True) + NORM_EPS) * ng_ref[...]
            dzh = dz[:, h * DN_DV:(h + 1) * DN_DV]
            o_ref[b, :, h * DN_DV:(h + 1) * DN_DV] = (on * _silu(dzh)).astype(BF16)

    @pl.when(c == n_c - 1)
    def _():
        nbuf_ref[...] = xp_scr[:, cs + off:cs + DN_HALO, :]
        sfin_ref[...] = s_scr[...]

    xp_scr[:, 0:DN_HALO, :] = xp_scr[:, cs:cs + DN_HALO, :]


def _deltanet(zmain3, zab3, buf, s0, cw, expa, dtb, ng, cs, nb):
    n_seq, seq_len, _ = zmain3.shape
    n_c = seq_len // cs
    cq, cz = Z_DNQKV // DN_CONV_CH, Z_DZ // DN_W
    row = pl.BlockSpec((1, LANES), lambda s, c: (0, 0))
    return pl.pallas_call(
        functools.partial(_dn_kernel, nb=nb, cs=cs, n_c=n_c),
        grid=(n_seq // nb, n_c),
        in_specs=[pl.BlockSpec((nb, cs, DN_CONV_CH), lambda s, c: (s, c, cq)),
                  pl.BlockSpec((nb, cs, DN_W), lambda s, c: (s, c, cz)),
                  pl.BlockSpec((nb, cs, LANES), lambda s, c: (s, c, 0)),
                  pl.BlockSpec((nb, DN_HALO, DN_CONV_CH), lambda s, c: (s, 0, 0)),
                  pl.BlockSpec((nb, DN_HEADS, DN_DK, DN_DV), lambda s, c: (s, 0, 0, 0)),
                  pl.BlockSpec((SHORT_CONV, DN_CONV_CH), lambda s, c: (0, 0)),
                  row, row, row],
        out_specs=[pl.BlockSpec((nb, cs, DN_W), lambda s, c: (s, c, 0)),
                   pl.BlockSpec((nb, SHORT_CONV - 1, DN_CONV_CH), lambda s, c: (s, 0, 0)),
                   pl.BlockSpec((nb, DN_HEADS, DN_DK, DN_DV), lambda s, c: (s, 0, 0, 0))],
        out_shape=[jax.ShapeDtypeStruct((n_seq, seq_len, DN_W), BF16),
                   jax.ShapeDtypeStruct((n_seq, SHORT_CONV - 1, DN_CONV_CH), F32),
                   jax.ShapeDtypeStruct((n_seq, DN_HEADS, DN_DK, DN_DV), F32)],
        scratch_shapes=[pltpu.VMEM((nb, DN_HALO + cs, DN_CONV_CH), F32),
                        pltpu.VMEM((nb, DN_HEADS, DN_DK, DN_DV), F32)],
        compiler_params=_cparams(("parallel", "arbitrary")),
        name="deltanet",
    )(zmain3, zmain3, zab3, buf, s0, cw, expa, dtb, ng)


def _merge_kernel(att_ref, cv_ref, dn_ref, g0_ref, g1_ref, g2_ref, x_ref, gm_ref, wb_ref, wo_ref, o_ref):
    acc = None
    for b, (br, gr) in enumerate(((att_ref, g0_ref), (cv_ref, g1_ref), (dn_ref, g2_ref))):
        yb = jnp.dot(br[...], wb_ref[b], preferred_element_type=F32)
        t = _sigmoid(gr[...].astype(F32)) * yb
        acc = t if acc is None else acc + t
    out = jnp.dot(acc.astype(BF16), wo_ref[...], preferred_element_type=F32)
    o_ref[...] = x_ref[...] + gm_ref[0] * out


def _merge(att, cv, dn, zmain, x, gate_m, wb, wo, tm, tps):
    m, d = x.shape
    br = pl.BlockSpec((tm, BRANCH_W), lambda i: (i, 0))
    return pl.pallas_call(
        _merge_kernel,
        grid=(m // tm,),
        in_specs=[br, br, br,
                  pl.BlockSpec((tm, d), lambda i: (i, 0)),
                  pl.BlockSpec((tm, d), lambda i: (i, 1)),
                  pl.BlockSpec((tm, d), lambda i: (i, 2)),
                  pl.BlockSpec((tm, d), lambda i: (i, 0)),
                  _mod_spec(gate_m, tm, tps),
                  pl.BlockSpec((N_BRANCH, BRANCH_W, d), lambda i: (0, 0, 0)),
                  pl.BlockSpec((d, d), lambda i: (0, 0))],
        out_specs=pl.BlockSpec((tm, d), lambda i: (i, 0)),
        out_shape=jax.ShapeDtypeStruct((m, d), F32),
        compiler_params=_cparams(("parallel",)),
        name="merge",
    )(att, cv, dn, zmain, zmain, zmain, x, gate_m, wb, wo)


def _finish(x_ref, gf_ref, fg_ref, acc, final):
    xn = x_ref[...] + gf_ref[0] * acc
    if final:
        ms = jnp.mean(xn * xn, axis=-1, keepdims=True)
        xn = xn * lax.rsqrt(ms + NORM_EPS) * fg_ref[...]
    return xn


def _ffn_kernel(x_ref, g_ref, sc_ref, sh_ref, gf_ref, fg_ref, wa_ref, wb_ref, wo_ref, o_ref, h_scr, acc_scr,
                *, n_f, final):
    j = pl.program_id(1)

    @pl.when(j == 0)
    def _():
        h_scr[...] = _mod_norm(x_ref[...], g_ref[...], sc_ref[0], sh_ref[0]).astype(BF16)
        acc_scr[...] = jnp.zeros_like(acc_scr)

    h = h_scr[...]
    a = jnp.dot(h, wa_ref[...], preferred_element_type=F32)
    b = jnp.dot(h, wb_ref[...], preferred_element_type=F32)
    acc_scr[...] += jnp.dot((_silu(a) * b).astype(BF16), wo_ref[...], preferred_element_type=F32)

    @pl.when(j == n_f - 1)
    def _():
        o_ref[...] = _finish(x_ref, gf_ref, fg_ref, acc_scr[...], final)


def _ffn(x, g, scale, shift, gate_f, fg, w_i, w_o, tm, tps, final):
    m, d = x.shape
    f = w_o.shape[0]
    tf = 256
    n_f = f // tf
    return pl.pallas_call(
        functools.partial(_ffn_kernel, n_f=n_f, final=final),
        grid=(m // tm, n_f),
        in_specs=[pl.BlockSpec((tm, d), lambda i, j: (i, 0)),
                  pl.BlockSpec((1, d), lambda i, j: (0, 0)),
                  _mod_spec(scale, tm, tps), _mod_spec(shift, tm, tps), _mod_spec(gate_f, tm, tps),
                  pl.BlockSpec((1, d), lambda i, j: (0, 0)),
                  pl.BlockSpec((d, tf), lambda i, j: (0, j)),
                  pl.BlockSpec((d, tf), lambda i, j: (0, j + n_f)),
                  pl.BlockSpec((tf, d), lambda i, j: (j, 0))],
        out_specs=pl.BlockSpec((tm, d), lambda i, j: (i, 0)),
        out_shape=jax.ShapeDtypeStruct((m, d), F32),
        scratch_shapes=[pltpu.VMEM((tm, d), BF16), pltpu.VMEM((tm, d), F32)],
        compiler_params=_cparams(("parallel", "arbitrary")),
        name="ffn",
    )(x, g, scale, shift, gate_f, fg, w_i, w_i, w_o)


def _top2_gate(logits, n_e):
    lane = lax.broadcasted_iota(jnp.int32, logits.shape, 1)
    lg = jnp.where(lane < n_e, logits, NEG_BIG)
    m1 = lg.max(axis=-1, keepdims=True)
    i1 = jnp.where(lg == m1, lane, LANES).min(axis=-1, keepdims=True)
    lg2 = jnp.where(lane == i1, NEG_BIG, lg)
    m2 = lg2.max(axis=-1, keepdims=True)
    i2 = jnp.where(lg2 == m2, lane, LANES).min(axis=-1, keepdims=True)
    e2 = jnp.exp(m2 - m1)
    w1 = 1.0 / (1.0 + e2)
    w2 = e2 / (1.0 + e2)
    return jnp.where(lane == i1, w1, 0.0) + jnp.where(lane == i2, w2, 0.0)


def _moe_kernel(x_ref, g_ref, sc_ref, sh_ref, gf_ref, fg_ref, wr_ref, wa_ref, wb_ref, wo_ref, o_ref,
                h_scr, acc_scr, gate_scr, *, n_e, n_f, final):
    e = pl.program_id(1)
    j = pl.program_id(2)

    @pl.when((e == 0) & (j == 0))
    def _():
        h = _mod_norm(x_ref[...], g_ref[...], sc_ref[0], sh_ref[0])
        h_scr[...] = h.astype(BF16)
        acc_scr[...] = jnp.zeros_like(acc_scr)
        logits = jnp.dot(h, wr_ref[...], preferred_element_type=F32, precision=lax.Precision.HIGHEST)
        gate_scr[...] = _top2_gate(logits, n_e)

    h = h_scr[...]
    gate = gate_scr[...]
    lane = lax.broadcasted_iota(jnp.int32, gate.shape, 1)
    ge = jnp.sum(jnp.where(lane == e, gate, 0.0), axis=-1, keepdims=True)
    a = jnp.dot(h, wa_ref[0], preferred_element_type=F32)
    b = jnp.dot(h, wb_ref[0], preferred_element_type=F32)
    act = (_silu(a) * b * ge).astype(BF16)
    acc_scr[...] += jnp.dot(act, wo_ref[0], preferred_element_type=F32)

    @pl.when((e == n_e - 1) & (j == n_f - 1))
    def _():
        o_ref[...] = _finish(x_ref, gf_ref, fg_ref, acc_scr[...], final)


def _moe(x, g, scale, shift, gate_f, fg, w_r, w_i, w_o, tm, tps, final):
    m, d = x.shape
    n_e, f, _ = w_o.shape
    tf = 256
    n_f = f // tf
    return pl.pallas_call(
        functools.partial(_moe_kernel, n_e=n_e, n_f=n_f, final=final),
        grid=(m // tm, n_e, n_f),
        in_specs=[pl.BlockSpec((tm, d), lambda i, e, j: (i, 0)),
                  pl.BlockSpec((1, d), lambda i, e, j: (0, 0)),
                  _mod_spec(scale, tm, tps), _mod_spec(shift, tm, tps), _mod_spec(gate_f, tm, tps),
                  pl.BlockSpec((1, d), lambda i, e, j: (0, 0)),
                  pl.BlockSpec((d, LANES), lambda i, e, j: (0, 0)),
                  pl.BlockSpec((1, d, tf), lambda i, e, j: (e, 0, j)),
                  pl.BlockSpec((1, d, tf), lambda i, e, j: (e, 0, j + n_f)),
                  pl.BlockSpec((1, tf, d), lambda i, e, j: (e, j, 0))],
        out_specs=pl.BlockSpec((tm, d), lambda i, e, j: (i, 0)),
        out_shape=jax.ShapeDtypeStruct((m, d), F32),
        scratch_shapes=[pltpu.VMEM((tm, d), BF16), pltpu.VMEM((tm, d), F32), pltpu.VMEM((tm, LANES), F32)],
        compiler_params=_cparams(("parallel", "arbitrary", "arbitrary")),
        name="moe",
    )(x, g, scale, shift, gate_f, fg, w_r, w_i, w_i, w_o)


def _prep_weights(w_in, w_branch, w_out, ffn_w_in, ffn_w_out, router_w, moe_w_in, moe_w_out):
    depth = w_in.shape[0]
    w_main = jnp.concatenate([w_in[..., OFF_GATE:], w_in[..., OFF_DN_QKV:OFF_DN_Z], w_in[..., OFF_AQ:OFF_GLU],
                              w_in[..., OFF_GLU:OFF_DN_QKV], w_in[..., OFF_DN_Z:OFF_DN_A]], axis=-1).astype(BF16)
    w_ab = jnp.pad(w_in[..., OFF_DN_A:OFF_GATE], ((0, 0), (0, 0), (0, LANES - 2 * DN_HEADS))).astype(BF16)
    w_r = jnp.pad(router_w, ((0, 0), (0, 0), (0, LANES - router_w.shape[-1])))
    return dict(w_main=w_main, w_ab=w_ab, w_branch=w_branch.astype(BF16), w_out=w_out.astype(BF16),
                ffn_w_in=ffn_w_in.astype(BF16), ffn_w_out=ffn_w_out.astype(BF16), w_r=w_r,
                moe_w_in=moe_w_in.astype(BF16), moe_w_out=moe_w_out.astype(BF16), depth=depth)


def _trunk(x3, mod, caches, pw, p, decode):
    n_seq, seq_len, d = x3.shape
    m = n_seq * seq_len
    x = x3.reshape(m, d)
    depth = pw["depth"]
    if decode:
        tm = tm_ffn = m
        tps = 1
    else:
        tm, tm_ffn = 512, 1024
        tps = seq_len // tm
    tps_ffn = seq_len // tm_ffn if not decode else 1

    def modv(l, idx, t):
        v = mod[l][:, idx]
        if decode:
            return jnp.repeat(v, seq_len, axis=0).reshape(m // t, t, d)
        return v.reshape(n_seq, 1, d)

    outs = ([], [], [], [], [])
    for l in range(depth):
        shift_m, scale_m, gate_m = (modv(l, i, tm) for i in range(3))
        zmain, zab = _in_proj(x, p["norm_g"][l, 0:1], modv(l, 1, tm_ffn), modv(l, 0, tm_ffn),
                              pw["w_main"][l], pw["w_ab"][l], tm_ffn, tps_ffn)
        if decode:
            cache_k, cache_v = caches[0][l], caches[1][l]
            nr = cache_k.shape[1]
            qpos = PAST_LEN + jnp.arange(seq_len)
            kpos = PAST_LEN - nr + jnp.arange(nr + seq_len)
            tab = _bias_table(p["rel_bias"][l], qpos, kpos)
            att = _attn_decode(zmain, cache_k.reshape(n_seq, nr, ATT_W), cache_v.reshape(n_seq, nr, ATT_W),
                               tab[:, :, :nr], tab[:, :, nr:], n_seq, seq_len)
            keep = seq_len
        else:
            qb, tq = 2 * CHUNK, N_BACK * CHUNK
            tab = _bias_table(p["rel_bias"][l], tq + jnp.arange(qb), jnp.arange(tq + qb))
            att = _attn_prompt(zmain, tab, n_seq, seq_len)
            keep = min(N_BACK * CHUNK, seq_len)
        z3 = zmain.reshape(n_seq, seq_len, Z_W)
        k_rows = z3[:, seq_len - keep:, Z_K:Z_K + ATT_W].astype(F32).reshape(n_seq, keep, ATT_HEADS, ATT_HEAD_DIM)
        v_rows = z3[:, seq_len - keep:, Z_V:Z_V + ATT_W].astype(F32).reshape(n_seq, keep, ATT_HEADS, ATT_HEAD_DIM)
        if decode:
            cbuf = jnp.pad(caches[2][l], ((0, 0), (HALO - (CONV_WIDTH - 1), 0), (0, 0)))
            t_rows, sub = seq_len, seq_len
        else:
            cbuf = jnp.zeros((n_seq, HALO, CONV_CH), F32)
            t_rows, sub = 512, 64
        cw = jnp.pad(p["conv_w"][l], ((0, HALO - CONV_WIDTH), (0, 0)))
        cv, new_cbuf = _conv_module(zmain, cbuf, cw, p["conv_b"][l][None], p["conv_ln_g"][l][None],
                                    p["conv_ln_b"][l][None], n_seq, seq_len, t_rows, sub)
        if decode:
            dbuf = jnp.pad(caches[3][l], ((0, 0), (DN_HALO - (SHORT_CONV - 1), 0), (0, 0)))
            s0 = caches[4][l]
        else:
            dbuf = jnp.zeros((n_seq, DN_HALO, DN_CONV_CH), F32)
            s0 = jnp.zeros((n_seq, DN_HEADS, DN_DK, DN_DV), F32)
        cs = seq_len if seq_len <= CHUNK else CHUNK
        lane_pad = (0, LANES - DN_HEADS)
        expa = jnp.pad(jnp.exp(p["dn_A_log"][l]), lane_pad)[None]
        dtb = jnp.pad(p["dn_dt_bias"][l], lane_pad)[None]
        dn, new_dbuf, s_new = _deltanet(z3, zab.reshape(n_seq, seq_len, LANES), dbuf, s0, p["dn_conv_w"][l],
                                        expa, dtb, p["dn_norm_g"][l][None], cs, 4)
        x = _merge(att, cv, dn.reshape(m, DN_W), zmain, x, gate_m, pw["w_branch"][l], pw["w_out"][l], tm, tps)
        for lst, s in zip(outs, (k_rows, v_rows, new_cbuf, new_dbuf, s_new)):
            lst.append(s)
        shift_f, scale_f, gate_f = (modv(l, i, tm_ffn) for i in range(3, 6))
        final = l == depth - 1
        fg = p["final_norm_g"][None]
        if l % 2 == 0:
            x = _ffn(x, p["norm_g"][l, 1:2], scale_f, shift_f, gate_f, fg, pw["ffn_w_in"][l // 2],
                     pw["ffn_w_out"][l // 2], tm_ffn, tps_ffn, final)
        else:
            x = _moe(x, p["norm_g"][l, 1:2], scale_f, shift_f, gate_f, fg, pw["w_r"][l // 2], pw["moe_w_in"][l // 2],
                     pw["moe_w_out"][l // 2], tm_ffn, tps_ffn, final)
    return x.reshape(n_seq, seq_len, d), tuple(jnp.stack(s) for s in outs)


def kernel(x_prompt, x_sample, c_prompt, c_sample, cache_attn_k, cache_attn_v, state_conv, state_dn_conv, state_dn,
           w_ada, b_ada, norm_g, w_in, rel_bias, conv_w, conv_b, conv_ln_g, conv_ln_b, dn_conv_w, dn_A_log,
           dn_dt_bias, dn_norm_g, w_branch, w_out, ffn_w_in, ffn_w_out, router_w, moe_w_in, moe_w_out,
           final_norm_g):
    depth = w_in.shape[0]
    d = x_prompt.shape[-1]
    n_p = c_prompt.shape[0]
    p = dict(norm_g=norm_g, rel_bias=rel_bias, conv_w=conv_w, conv_b=conv_b, conv_ln_g=conv_ln_g,
             conv_ln_b=conv_ln_b, dn_conv_w=dn_conv_w, dn_A_log=dn_A_log, dn_dt_bias=dn_dt_bias,
             dn_norm_g=dn_norm_g, final_norm_g=final_norm_g)
    pw = _prep_weights(w_in, w_branch, w_out, ffn_w_in, ffn_w_out, router_w, moe_w_in, moe_w_out)
    mod = _ada(jnp.concatenate([c_prompt, c_sample], axis=0), w_ada, b_ada)
    mod = mod.reshape(depth, -1, 6, d)
    y_p, st_p = _trunk(x_prompt, mod[:, :n_p], None, pw, p, decode=False)
    y_s, st_s = _trunk(x_sample, mod[:, n_p:], (cache_attn_k, cache_attn_v, state_conv, state_dn_conv, state_dn),
                       pw, p, decode=True)
    return (y_p, y_s) + st_p + st_s
```

```python
import functools

import numpy as np
import jax
import jax.numpy as jnp
from jax import lax
from jax.experimental import pallas as pl
from jax.experimental.pallas import tpu as pltpu

F32 = jnp.float32
BF16 = jnp.bfloat16

PAST_LEN = 4096
CHUNK = 64
N_BACK = 8
ATT_HEADS = 8
ATT_HEAD_DIM = 64
ATT_W = ATT_HEADS * ATT_HEAD_DIM
REL_CLIP = 128
CONV_CH = 512
CONV_WIDTH = 31
DN_HEADS = 4
DN_DK = 128
DN_DV = 128
DN_QK_W = DN_HEADS * DN_DK
DN_W = DN_HEADS * DN_DV
DN_CONV_CH = 2 * DN_QK_W + DN_W
SHORT_CONV = 4
N_BRANCH = 3
BRANCH_W = 512
TOP_K = 2
NORM_EPS = 1e-6
NEG_BIG = -1e30

OFF_AQ = 0
OFF_GLU = 3 * ATT_W
OFF_DN_QKV = OFF_GLU + 2 * CONV_CH
OFF_DN_Z = OFF_DN_QKV + DN_CONV_CH
OFF_DN_A = OFF_DN_Z + DN_W
OFF_GATE = OFF_DN_A + 2 * DN_HEADS

Z_GATE = 0
Z_DNQKV = 3072
Z_Q = 4608
Z_K = 5120
Z_V = 5632
Z_U1 = 6144
Z_U2 = 6656
Z_DZ = 7168
Z_W = 7680
LANES = 128
HALO = 32
DN_HALO = 8

VMEM_LIMIT = 56 * 1024 * 1024


def _cparams(sem):
    return pltpu.CompilerParams(dimension_semantics=sem, vmem_limit_bytes=VMEM_LIMIT)


def _sigmoid(x):
    return 1.0 / (1.0 + jnp.exp(-x))


def _silu(x):
    return x * _sigmoid(x)


def _mod_norm(x, g, scale, shift):
    ms = jnp.mean(x * x, axis=-1, keepdims=True)
    return (x * lax.rsqrt(ms + NORM_EPS) * g) * (1.0 + scale) + shift


def _ada_kernel(c_ref, w_ref, b_ref, o_ref):
    s = _silu(c_ref[...])
    o_ref[0] = jnp.dot(s.astype(BF16), w_ref[0].astype(BF16), preferred_element_type=F32) + b_ref[0]


def _ada(c_all, w_ada, b_ada):
    depth, d, n6 = w_ada.shape
    n = c_all.shape[0]
    tn = 1024
    return pl.pallas_call(
        _ada_kernel,
        grid=(depth, n6 // tn),
        in_specs=[pl.BlockSpec((n, d), lambda l, j: (0, 0)),
                  pl.BlockSpec((1, d, tn), lambda l, j: (l, 0, j)),
                  pl.BlockSpec((1, 1, tn), lambda l, j: (l, 0, j))],
        out_specs=pl.BlockSpec((1, n, tn), lambda l, j: (l, 0, j)),
        out_shape=jax.ShapeDtypeStruct((depth, n, n6), F32),
        compiler_params=_cparams(("parallel", "parallel")),
        name="ada_mod",
    )(c_all, w_ada, b_ada.reshape(depth, 1, n6))


def _in_proj_kernel(x_ref, g_ref, sc_ref, sh_ref, w_ref, wab_ref, z_ref, zab_ref, h_scr):
    @pl.when(pl.program_id(1) == 0)
    def _():
        h = _mod_norm(x_ref[...], g_ref[...], sc_ref[0], sh_ref[0]).astype(BF16)
        h_scr[...] = h
        zab_ref[...] = jnp.dot(h, wab_ref[...], preferred_element_type=F32)

    z_ref[...] = jnp.dot(h_scr[...], w_ref[...], preferred_element_type=F32).astype(BF16)


def _mod_spec(mod, tm, tps):
    r = mod.shape[1]
    if r == 1:
        return pl.BlockSpec((1, 1, mod.shape[2]), lambda i, *_: (i // tps, 0, 0))
    return pl.BlockSpec((1, r, mod.shape[2]), lambda i, *_: (i, 0, 0))


def _in_proj(x, g, scale, shift, w_main, w_ab, tm, tps):
    m, d = x.shape
    tn = 512
    nt = Z_W // tn
    return pl.pallas_call(
        _in_proj_kernel,
        grid=(m // tm, nt),
        in_specs=[pl.BlockSpec((tm, d), lambda i, j: (i, 0)),
                  pl.BlockSpec((1, d), lambda i, j: (0, 0)),
                  _mod_spec(scale, tm, tps), _mod_spec(shift, tm, tps),
                  pl.BlockSpec((d, tn), lambda i, j: (0, j)),
                  pl.BlockSpec((d, LANES), lambda i, j: (0, 0))],
        out_specs=[pl.BlockSpec((tm, tn), lambda i, j: (i, j)),
                   pl.BlockSpec((tm, LANES), lambda i, j: (i, 0))],
        out_shape=[jax.ShapeDtypeStruct((m, Z_W), BF16), jax.ShapeDtypeStruct((m, LANES), F32)],
        scratch_shapes=[pltpu.VMEM((tm, d), BF16)],
        compiler_params=_cparams(("parallel", "arbitrary")),
        name="in_proj",
    )(x, g, scale, shift, w_main, w_ab)


def _softmax_pv(parts, vs):
    m = parts[0].max(axis=-1, keepdims=True)
    for s in parts[1:]:
        m = jnp.maximum(m, s.max(axis=-1, keepdims=True))
    num = None
    den = None
    for s, v in zip(parts, vs):
        p = jnp.exp(s - m)
        l = p.sum(axis=-1, keepdims=True)
        o = jnp.dot(p.astype(BF16), v, preferred_element_type=F32)
        num = o if num is None else num + o
        den = l if den is None else den + l
    return num / den


def _qk(q, k):
    return lax.dot_general(q, k, (((1,), (1,)), ((), ())), preferred_element_type=F32)


def _attn_prompt_kernel(q_ref, kp_ref, kc_ref, vp_ref, vc_ref, tab_ref, o_ref, *, tps, tq, qb, win):
    first = (pl.program_id(0) % tps == 0).astype(F32)
    kwin = jnp.concatenate([kp_ref[...], kc_ref[...]], axis=0)
    vwin = jnp.concatenate([vp_ref[...], vc_ref[...]], axis=0)
    rowid = lax.broadcasted_iota(jnp.int32, (1, 2 * tq), 1)
    neg = jnp.where(rowid < tq, first * NEG_BIG, 0.0)
    lo = lax.broadcasted_iota(jnp.int32, (qb, LANES), 1) < ATT_HEAD_DIM
    scale = ATT_HEAD_DIM ** -0.5
    for b in range(tq // qb):
        negw = neg[:, b * qb:b * qb + win]
        for hp in range(ATT_HEADS // 2):
            cols = slice(hp * LANES, (hp + 1) * LANES)
            q = q_ref[b * qb:(b + 1) * qb, cols]
            kw = kwin[b * qb:b * qb + win, cols]
            vw = vwin[b * qb:b * qb + win, cols]
            outs = []
            for half in range(2):
                qm = jnp.where(lo if half == 0 else jnp.logical_not(lo), q, jnp.zeros_like(q))
                s = _qk(qm, kw) * scale + tab_ref[2 * hp + half] + negw
                outs.append(_softmax_pv([s], [vw]))
            o_ref[b * qb:(b + 1) * qb, cols] = jnp.where(lo, outs[0], outs[1]).astype(BF16)


def _attn_prompt(zmain, tab, n_seq, seq_len):
    m = zmain.shape[0]
    tq = N_BACK * CHUNK
    qb = 2 * CHUNK
    win = tq + qb
    tps = seq_len // tq
    cq, ck, cv = Z_Q // ATT_W, Z_K // ATT_W, Z_V // ATT_W

    def prev(i):
        return jnp.where(i % tps == 0, i, i - 1)

    return pl.pallas_call(
        functools.partial(_attn_prompt_kernel, tps=tps, tq=tq, qb=qb, win=win),
        grid=(m // tq,),
        in_specs=[pl.BlockSpec((tq, ATT_W), lambda i: (i, cq)),
                  pl.BlockSpec((tq, ATT_W), lambda i: (prev(i), ck)),
                  pl.BlockSpec((tq, ATT_W), lambda i: (i, ck)),
                  pl.BlockSpec((tq, ATT_W), lambda i: (prev(i), cv)),
                  pl.BlockSpec((tq, ATT_W), lambda i: (i, cv)),
                  pl.BlockSpec((ATT_HEADS, qb, win), lambda i: (0, 0, 0))],
        out_specs=pl.BlockSpec((tq, ATT_W), lambda i: (i, 0)),
        out_shape=jax.ShapeDtypeStruct((m, ATT_W), BF16),
        compiler_params=_cparams(("parallel",)),
        name="attn_prompt",
    )(zmain, zmain, zmain, zmain, zmain, tab)


def _attn_decode_kernel(q_ref, kc_ref, kn_ref, vc_ref, vn_ref, tabc_ref, tabn_ref, o_ref):
    lq = q_ref.shape[0]
    kc = kc_ref[0].astype(BF16)
    vc = vc_ref[0].astype(BF16)
    lo = lax.broadcasted_iota(jnp.int32, (lq, LANES), 1) < ATT_HEAD_DIM
    scale = ATT_HEAD_DIM ** -0.5
    for hp in range(ATT_HEADS // 2):
        cols = slice(hp * LANES, (hp + 1) * LANES)
        q = q_ref[:, cols]
        outs = []
        for half in range(2):
            qm = jnp.where(lo if half == 0 else jnp.logical_not(lo), q, jnp.zeros_like(q))
            s1 = _qk(qm, kc[:, cols]) * scale + tabc_ref[2 * hp + half]
            s2 = _qk(qm, kn_ref[:, cols]) * scale + tabn_ref[2 * hp + half]
            outs.append(_softmax_pv([s1, s2], [vc[:, cols], vn_ref[:, cols]]))
        o_ref[:, cols] = jnp.where(lo, outs[0], outs[1]).astype(BF16)


def _attn_decode(zmain, cache_k, cache_v, tab_c, tab_n, n_seq, lq):
    m = zmain.shape[0]
    nr = cache_k.shape[1]
    cq, ck, cv = Z_Q // ATT_W, Z_K // ATT_W, Z_V // ATT_W
    return pl.pallas_call(
        _attn_decode_kernel,
        grid=(n_seq,),
        in_specs=[pl.BlockSpec((lq, ATT_W), lambda i: (i, cq)),
                  pl.BlockSpec((1, nr, ATT_W), lambda i: (i, 0, 0)),
                  pl.BlockSpec((lq, ATT_W), lambda i: (i, ck)),
                  pl.BlockSpec((1, nr, ATT_W), lambda i: (i, 0, 0)),
                  pl.BlockSpec((lq, ATT_W), lambda i: (i, cv)),
                  pl.BlockSpec((ATT_HEADS, lq, nr), lambda i: (0, 0, 0)),
                  pl.BlockSpec((ATT_HEADS, lq, lq), lambda i: (0, 0, 0))],
        out_specs=pl.BlockSpec((lq, ATT_W), lambda i: (i, 0)),
        out_shape=jax.ShapeDtypeStruct((m, ATT_W), BF16),
        compiler_params=_cparams(("parallel",)),
        name="attn_decode",
    )(zmain, cache_k, zmain, cache_v, zmain, tab_c, tab_n)


def _bias_table(rel_bias_l, qpos0, nq, kpos0, nk):
    qpos = qpos0 + np.arange(nq)
    kpos = kpos0 + np.arange(nk)
    qc = (qpos // CHUNK)[:, None]
    kc = (kpos // CHUNK)[None, :]
    ok = (kpos[None, :] >= 0) & (kc <= qc) & (kc >= qc - N_BACK)
    t_min = qpos0 - (kpos0 + nk - 1)
    t_max = qpos0 + nq - 1 - kpos0
    idx = np.clip(np.arange(t_max, t_min - 1, -1), -REL_CLIP, REL_CLIP) + REL_CLIP
    grev = rel_bias_l.astype(F32)[:, idx]
    rows = [grev[:, nq - 1 - i:nq - 1 - i + nk] for i in range(nq)]
    return jnp.where(ok[None], jnp.stack(rows, axis=1), NEG_BIG)


def _conv_kernel(u1_ref, u2_ref, buf_ref, w_ref, cb_ref, lg_ref, lb_ref, o_ref, nb_ref, xp_scr, *, t_rows, sub, n_t):
    t = pl.program_id(1)

    @pl.when(t == 0)
    def _():
        xp_scr[0:HALO, :] = buf_ref[0]

    u1 = u1_ref[...].astype(F32)
    u2 = u2_ref[...].astype(F32)
    xp_scr[HALO:HALO + t_rows, :] = u1 * _sigmoid(u2)
    off = HALO - (CONV_WIDTH - 1)
    for r0 in range(0, t_rows, sub):
        acc = w_ref[0:1, :] * xp_scr[r0 + off:r0 + off + sub, :]
        for j in range(1, CONV_WIDTH):
            acc = acc + w_ref[j:j + 1, :] * xp_scr[r0 + off + j:r0 + off + j + sub, :]
        cv = acc + cb_ref[...]
        mu = jnp.mean(cv, axis=-1, keepdims=True)
        cen = cv - mu
        var = jnp.mean(cen * cen, axis=-1, keepdims=True)
        y = cen * lax.rsqrt(var + NORM_EPS) * lg_ref[...] + lb_ref[...]
        o_ref[r0:r0 + sub, :] = _silu(y).astype(BF16)

    @pl.when(t == n_t - 1)
    def _():
        nb_ref[0] = xp_scr[t_rows + off:t_rows + HALO, :]

    xp_scr[0:HALO, :] = xp_scr[t_rows:t_rows + HALO, :]


def _conv_module(zmain, buf, w, cb, lg, lb, n_seq, seq_len, t_rows, sub):
    m = zmain.shape[0]
    n_t = seq_len // t_rows
    c1, c2 = Z_U1 // CONV_CH, Z_U2 // CONV_CH
    vec = pl.BlockSpec((1, CONV_CH), lambda n, t: (0, 0))
    return pl.pallas_call(
        functools.partial(_conv_kernel, t_rows=t_rows, sub=sub, n_t=n_t),
        grid=(n_seq, n_t),
        in_specs=[pl.BlockSpec((t_rows, CONV_CH), lambda n, t: (n * n_t + t, c1)),
                  pl.BlockSpec((t_rows, CONV_CH), lambda n, t: (n * n_t + t, c2)),
                  pl.BlockSpec((1, HALO, CONV_CH), lambda n, t: (n, 0, 0)),
                  pl.BlockSpec((HALO, CONV_CH), lambda n, t: (0, 0)),
                  vec, vec, vec],
        out_specs=[pl.BlockSpec((t_rows, CONV_CH), lambda n, t: (n * n_t + t, 0)),
                   pl.BlockSpec((1, CONV_WIDTH - 1, CONV_CH), lambda n, t: (n, 0, 0))],
        out_shape=[jax.ShapeDtypeStruct((m, CONV_CH), BF16),
                   jax.ShapeDtypeStruct((n_seq, CONV_WIDTH - 1, CONV_CH), F32)],
        scratch_shapes=[pltpu.VMEM((HALO + t_rows, CONV_CH), F32)],
        compiler_params=_cparams(("parallel", "arbitrary")),
        name="conv_module",
    )(zmain, zmain, buf, w, cb, lg, lb)


def _split3(x):
    hi = x.astype(BF16)
    r = x - hi.astype(F32)
    mid = r.astype(BF16)
    lo = (r - mid.astype(F32)).astype(BF16)
    return hi, mid, lo


def _dn_kernel(qkv_ref, dz_ref, ab_ref, buf_ref, s0_ref, cw_ref, expa_ref, dtb_ref, ng_ref,
               o_ref, nbuf_ref, sfin_ref, xp_scr, s_scr, *, nb, cs, n_c):
    c = pl.program_id(1)

    @pl.when(c == 0)
    def _():
        xp_scr[:, 0:DN_HALO, :] = buf_ref[...]
        s_scr[...] = s0_ref[...]

    lg = cs.bit_length() - 1
    wd = DN_HEADS * cs
    ii = lax.broadcasted_iota(jnp.int32, (cs, cs), 0)
    jj = lax.broadcasted_iota(jnp.int32, (cs, cs), 1)
    lincl = jnp.where(ii >= jj, 1.0, 0.0).astype(BF16)
    r = lax.broadcasted_iota(jnp.int32, (cs, wd), 0)
    col = lax.broadcasted_iota(jnp.int32, (cs, wd), 1)
    jl = col & (cs - 1)
    hid = col >> lg
    incl_s = r >= jl
    strict_s = r > jl
    diag_s = r == jl
    bdm = (lax.broadcasted_iota(jnp.int32, (wd, wd), 0) >> lg) == (lax.broadcasted_iota(jnp.int32, (wd, wd), 1) >> lg)
    kbm = (lax.broadcasted_iota(jnp.int32, (wd, DN_QK_W), 0) >> lg) == (
        lax.broadcasted_iota(jnp.int32, (wd, DN_QK_W), 1) // DN_DK)

    def bd(m):
        return jnp.where(bdm, jnp.concatenate([m] * DN_HEADS, axis=0), 0.0).astype(BF16)

    def level_sel(sh):
        bi = r >> sh
        return jnp.where((bi & 1) == 1, bi - 1, -1) == (jl >> sh)

    off = DN_HALO - (SHORT_CONV - 1)
    heads = range(DN_HEADS)
    for b in range(nb):
        xp_scr[b, DN_HALO:DN_HALO + cs, :] = qkv_ref[b].astype(F32)
        y = cw_ref[0:1, :] * xp_scr[b, off:off + cs, :]
        for j in range(1, SHORT_CONV):
            y = y + cw_ref[j:j + 1, :] * xp_scr[b, off + j:off + j + cs, :]
        y = _silu(y)
        ab = ab_ref[b]
        xg = ab + dtb_ref[...]
        softplus = jnp.maximum(xg, 0.0) + jnp.log(1.0 + jnp.exp(-jnp.abs(xg)))
        gfull = -expa_ref[...] * softplus
        beta_full = _sigmoid(ab)
        gc3 = jnp.dot(lincl, jnp.concatenate(_split3(gfull), axis=1), preferred_element_type=F32)
        gcum = gc3[:, :LANES] + gc3[:, LANES:2 * LANES] + gc3[:, 2 * LANES:]
        dz = dz_ref[b].astype(F32)
        q, k, kb, kbg, vb, qg, kd, gl = [], [], [], [], [], [], [], []
        gcx = None
        for h in heads:
            qh = y[:, h * DN_DK:(h + 1) * DN_DK]
            kh = y[:, DN_QK_W + h * DN_DK:DN_QK_W + (h + 1) * DN_DK]
            vh = y[:, 2 * DN_QK_W + h * DN_DV:2 * DN_QK_W + (h + 1) * DN_DV]
            qh = qh * (lax.rsqrt(jnp.sum(qh * qh, axis=-1, keepdims=True) + 1e-6) * (DN_DK ** -0.5))
            kh = kh * lax.rsqrt(jnp.sum(kh * kh, axis=-1, keepdims=True) + 1e-6)
            gc = gcum[:, h:h + 1]
            beta = beta_full[:, DN_HEADS + h:DN_HEADS + h + 1]
            glast = gcum[cs - 1:cs, h:h + 1]
            eg = jnp.exp(gc)
            gcx = gc if gcx is None else jnp.where(hid == h, gc, gcx)
            q.append(qh)
            k.append(kh)
            kb.append(kh * beta)
            kbg.append(kh * (beta * eg))
            vb.append(vh * beta)
            qg.append(qh * eg)
            kd.append(kh * jnp.exp(glast - gc))
            gl.append(jnp.exp(glast))
        gcx = jnp.broadcast_to(gcx, (cs, wd))
        grx = jnp.sum(jnp.where(diag_s, gcx, 0.0), axis=0, keepdims=True)
        decay = jnp.exp(jnp.where(incl_s, gcx - grx, NEG_BIG))
        x_all = jnp.concatenate([jnp.concatenate(kb, axis=1), jnp.concatenate(q, axis=1)], axis=0).astype(BF16)
        k_bd = jnp.where(kbm, jnp.concatenate([jnp.concatenate(k, axis=1)] * DN_HEADS, axis=0), 0.0).astype(BF16)
        kk = _qk(x_all, k_bd)
        a = jnp.where(strict_s, kk[:cs] * decay, 0.0)
        qk = kk[cs:] * decay
        d = jnp.where(diag_s, 1.0, 0.0) - jnp.where(level_sel(0), a, 0.0)
        for sh in range(1, lg):
            x = jnp.dot(d.astype(BF16), bd(jnp.where(level_sel(sh), a, 0.0)), preferred_element_type=F32)
            d = d - jnp.dot(x.astype(BF16), bd(d), preferred_element_type=F32)
        rhs = jnp.concatenate([jnp.concatenate([kbg[h], vb[h]], axis=1) for h in heads], axis=0).astype(BF16)
        wu = jnp.dot(bd(d), rhs, preferred_element_type=F32)
        s_old, o1, v_new = [], [], []
        for h in heads:
            s_old.append(s_scr[b, h])
            w_h = wu[h * cs:(h + 1) * cs, :DN_DK]
            wq = jnp.dot(jnp.concatenate([w_h, qg[h]], axis=0).astype(BF16), s_old[h].astype(BF16),
                         preferred_element_type=F32)
            v_new.append(wu[h * cs:(h + 1) * cs, DN_DK:] - wq[:cs])
            o1.append(wq[cs:])
        o2 = jnp.dot(bd(qk), jnp.concatenate(v_new, axis=0).astype(BF16), preferred_element_type=F32)
        for h in heads:
            o = o1[h] + o2[h * cs:(h + 1) * cs]
            s_scr[b, h] = s_old[h] * gl[h] + lax.dot_general(
                kd[h].astype(BF16), v_new[h].astype(BF16), (((0,), (0,)), ((), ())), preferred_element_type=F32)
            on = o * lax.rsqrt(jnp.mean(o * o, axis=-1, keepdims=True) + NORM_EPS) * ng_ref[...]
            dzh = dz[:, h * DN_DV:(h + 1) * DN_DV]
            o_ref[b, :, h * DN_DV:(h + 1) * DN_DV] = (on * _silu(dzh)).astype(BF16)

    @pl.when(c == n_c - 1)
    def _():
        nbuf_ref[...] = xp_scr[:, cs + off:cs + DN_HALO, :]
        sfin_ref[...] = s_scr[...]

    xp_scr[:, 0:DN_HALO, :] = xp_scr[:, cs:cs + DN_HALO, :]


def _deltanet(zmain3, zab3, buf, s0, cw, expa, dtb, ng, cs, nb):
    n_seq, seq_len, _ = zmain3.shape
    n_c = seq_len // cs
    cq, cz = Z_DNQKV // DN_CONV_CH, Z_DZ // DN_W
    row = pl.BlockSpec((1, LANES), lambda s, c: (0, 0))
    return pl.pallas_call(
        functools.partial(_dn_kernel, nb=nb, cs=cs, n_c=n_c),
        grid=(n_seq // nb, n_c),
        in_specs=[pl.BlockSpec((nb, cs, DN_CONV_CH), lambda s, c: (s, c, cq)),
                  pl.BlockSpec((nb, cs, DN_W), lambda s, c: (s, c, cz)),
                  pl.BlockSpec((nb, cs, LANES), lambda s, c: (s, c, 0)),
                  pl.BlockSpec((nb, DN_HALO, DN_CONV_CH), lambda s, c: (s, 0, 0)),
                  pl.BlockSpec((nb, DN_HEADS, DN_DK, DN_DV), lambda s, c: (s, 0, 0, 0)),
                  pl.BlockSpec((SHORT_CONV, DN_CONV_CH), lambda s, c: (0, 0)),
                  row, row, row],
        out_specs=[pl.BlockSpec((nb, cs, DN_W), lambda s, c: (s, c, 0)),
                   pl.BlockSpec((nb, SHORT_CONV - 1, DN_CONV_CH), lambda s, c: (s, 0, 0)),
                   pl.BlockSpec((nb, DN_HEADS, DN_DK, DN_DV), lambda s, c: (s, 0, 0, 0))],
        out_shape=[jax.ShapeDtypeStruct((n_seq, seq_len, DN_W), BF16),
                   jax.ShapeDtypeStruct((n_seq, SHORT_CONV - 1, DN_CONV_CH), F32),
                   jax.ShapeDtypeStruct((n_seq, DN_HEADS, DN_DK, DN_DV), F32)],
        scratch_shapes=[pltpu.VMEM((nb, DN_HALO + cs, DN_CONV_CH), F32),
                        pltpu.VMEM((nb, DN_HEADS, DN_DK, DN_DV), F32)],
        compiler_params=_cparams(("parallel", "arbitrary")),
        name="deltanet",
    )(zmain3, zmain3, zab3, buf, s0, cw, expa, dtb, ng)


def _merge_kernel(att_ref, cv_ref, dn_ref, g0_ref, g1_ref, g2_ref, x_ref, gm_ref, wb_ref, wo_ref, o_ref):
    acc = None
    for b, (br, gr) in enumerate(((att_ref, g0_ref), (cv_ref, g1_ref), (dn_ref, g2_ref))):
        yb = jnp.dot(br[...], wb_ref[b], preferred_element_type=F32)
        t = _sigmoid(gr[...].astype(F32)) * yb
        acc = t if acc is None else acc + t
    out = jnp.dot(acc.astype(BF16), wo_ref[...], preferred_element_type=F32)
    o_ref[...] = x_ref[...] + gm_ref[0] * out


def _merge(att, cv, dn, zmain, x, gate_m, wb, wo, tm, tps):
    m, d = x.shape
    br = pl.BlockSpec((tm, BRANCH_W), lambda i: (i, 0))
    return pl.pallas_call(
        _merge_kernel,
        grid=(m // tm,),
        in_specs=[br, br, br,
                  pl.BlockSpec((tm, d), lambda i: (i, 0)),
                  pl.BlockSpec((tm, d), lambda i: (i, 1)),
                  pl.BlockSpec((tm, d), lambda i: (i, 2)),
                  pl.BlockSpec((tm, d), lambda i: (i, 0)),
                  _mod_spec(gate_m, tm, tps),
                  pl.BlockSpec((N_BRANCH, BRANCH_W, d), lambda i: (0, 0, 0)),
                  pl.BlockSpec((d, d), lambda i: (0, 0))],
        out_specs=pl.BlockSpec((tm, d), lambda i: (i, 0)),
        out_shape=jax.ShapeDtypeStruct((m, d), F32),
        compiler_params=_cparams(("parallel",)),
        name="merge",
    )(att, cv, dn, zmain, zmain, zmain, x, gate_m, wb, wo)


def _finish(x_ref, gf_ref, fg_ref, acc, final):
    xn = x_ref[...] + gf_ref[0] * acc
    if final:
        ms = jnp.mean(xn * xn, axis=-1, keepdims=True)
        xn = xn * lax.rsqrt(ms + NORM_EPS) * fg_ref[...]
    return xn


def _ffn_kernel(x_ref, g_ref, sc_ref, sh_ref, gf_ref, fg_ref, wa_ref, wb_ref, wo_ref, o_ref, h_scr, acc_scr,
                *, n_f, final):
    j = pl.program_id(1)

    @pl.when(j == 0)
    def _():
        h_scr[...] = _mod_norm(x_ref[...], g_ref[...], sc_ref[0], sh_ref[0]).astype(BF16)
        acc_scr[...] = jnp.zeros_like(acc_scr)

    h = h_scr[...]
    a = jnp.dot(h, wa_ref[...], preferred_element_type=F32)
    b = jnp.dot(h, wb_ref[...], preferred_element_type=F32)
    acc_scr[...] += jnp.dot((_silu(a) * b).astype(BF16), wo_ref[...], preferred_element_type=F32)

    @pl.when(j == n_f - 1)
    def _():
        o_ref[...] = _finish(x_ref, gf_ref, fg_ref, acc_scr[...], final)


def _ffn(x, g, scale, shift, gate_f, fg, w_i, w_o, tm, tps, final):
    m, d = x.shape
    f = w_o.shape[0]
    tf = 256
    n_f = f // tf
    return pl.pallas_call(
        functools.partial(_ffn_kernel, n_f=n_f, final=final),
        grid=(m // tm, n_f),
        in_specs=[pl.BlockSpec((tm, d), lambda i, j: (i, 0)),
                  pl.BlockSpec((1, d), lambda i, j: (0, 0)),
                  _mod_spec(scale, tm, tps), _mod_spec(shift, tm, tps), _mod_spec(gate_f, tm, tps),
                  pl.BlockSpec((1, d), lambda i, j: (0, 0)),
                  pl.BlockSpec((d, tf), lambda i, j: (0, j)),
                  pl.BlockSpec((d, tf), lambda i, j: (0, j + n_f)),
                  pl.BlockSpec((tf, d), lambda i, j: (j, 0))],
        out_specs=pl.BlockSpec((tm, d), lambda i, j: (i, 0)),
        out_shape=jax.ShapeDtypeStruct((m, d), F32),
        scratch_shapes=[pltpu.VMEM((tm, d), BF16), pltpu.VMEM((tm, d), F32)],
        compiler_params=_cparams(("parallel", "arbitrary")),
        name="ffn",
    )(x, g, scale, shift, gate_f, fg, w_i, w_i, w_o)


def _top2_gate(logits, n_e):
    lane = lax.broadcasted_iota(jnp.int32, logits.shape, 1)
    lg = jnp.where(lane < n_e, logits, NEG_BIG)
    m1 = lg.max(axis=-1, keepdims=True)
    i1 = jnp.where(lg == m1, lane, LANES).min(axis=-1, keepdims=True)
    lg2 = jnp.where(lane == i1, NEG_BIG, lg)
    m2 = lg2.max(axis=-1, keepdims=True)
    i2 = jnp.where(lg2 == m2, lane, LANES).min(axis=-1, keepdims=True)
    e2 = jnp.exp(m2 - m1)
    w1 = 1.0 / (1.0 + e2)
    w2 = e2 / (1.0 + e2)
    return jnp.where(lane == i1, w1, 0.0) + jnp.where(lane == i2, w2, 0.0)


def _moe_kernel(x_ref, g_ref, sc_ref, sh_ref, gf_ref, fg_ref, wr_ref, wa_ref, wb_ref, wo_ref, o_ref,
                h_scr, acc_scr, gate_scr, *, n_e, n_f, final):
    e = pl.program_id(1)
    j = pl.program_id(2)

    @pl.when((e == 0) & (j == 0))
    def _():
        h = _mod_norm(x_ref[...], g_ref[...], sc_ref[0], sh_ref[0])
        h_scr[...] = h.astype(BF16)
        acc_scr[...] = jnp.zeros_like(acc_scr)
        logits = jnp.dot(h, wr_ref[...], preferred_element_type=F32, precision=lax.Precision.HIGHEST)
        gate_scr[...] = _top2_gate(logits, n_e)

    h = h_scr[...]
    gate = gate_scr[...]
    lane = lax.broadcasted_iota(jnp.int32, gate.shape, 1)
    ge = jnp.sum(jnp.where(lane == e, gate, 0.0), axis=-1, keepdims=True)
    a = jnp.dot(h, wa_ref[0], preferred_element_type=F32)
    b = jnp.dot(h, wb_ref[0], preferred_element_type=F32)
    act = (_silu(a) * b * ge).astype(BF16)
    acc_scr[...] += jnp.dot(act, wo_ref[0], preferred_element_type=F32)

    @pl.when((e == n_e - 1) & (j == n_f - 1))
    def _():
        o_ref[...] = _finish(x_ref, gf_ref, fg_ref, acc_scr[...], final)


def _moe(x, g, scale, shift, gate_f, fg, w_r, w_i, w_o, tm, tps, final):
    m, d = x.shape
    n_e, f, _ = w_o.shape
    tf = 256
    n_f = f // tf
    return pl.pallas_call(
        functools.partial(_moe_kernel, n_e=n_e, n_f=n_f, final=final),
        grid=(m // tm, n_e, n_f),
        in_specs=[pl.BlockSpec((tm, d), lambda i, e, j: (i, 0)),
                  pl.BlockSpec((1, d), lambda i, e, j: (0, 0)),
                  _mod_spec(scale, tm, tps), _mod_spec(shift, tm, tps), _mod_spec(gate_f, tm, tps),
                  pl.BlockSpec((1, d), lambda i, e, j: (0, 0)),
                  pl.BlockSpec((d, LANES), lambda i, e, j: (0, 0)),
                  pl.BlockSpec((1, d, tf), lambda i, e, j: (e, 0, j)),
                  pl.BlockSpec((1, d, tf), lambda i, e, j: (e, 0, j + n_f)),
                  pl.BlockSpec((1, tf, d), lambda i, e, j: (e, j, 0))],
        out_specs=pl.BlockSpec((tm, d), lambda i, e, j: (i, 0)),
        out_shape=jax.ShapeDtypeStruct((m, d), F32),
        scratch_shapes=[pltpu.VMEM((tm, d), BF16), pltpu.VMEM((tm, d), F32), pltpu.VMEM((tm, LANES), F32)],
        compiler_params=_cparams(("parallel", "arbitrary", "arbitrary")),
        name="moe",
    )(x, g, scale, shift, gate_f, fg, w_r, w_i, w_i, w_o)


def _prep_weights(w_in, w_branch, w_out, ffn_w_in, ffn_w_out, router_w, moe_w_in, moe_w_out):
    depth = w_in.shape[0]
    w_main = jnp.concatenate([w_in[..., OFF_GATE:], w_in[..., OFF_DN_QKV:OFF_DN_Z], w_in[..., OFF_AQ:OFF_GLU],
                              w_in[..., OFF_GLU:OFF_DN_QKV], w_in[..., OFF_DN_Z:OFF_DN_A]], axis=-1).astype(BF16)
    w_ab = jnp.pad(w_in[..., OFF_DN_A:OFF_GATE], ((0, 0), (0, 0), (0, LANES - 2 * DN_HEADS))).astype(BF16)
    w_r = jnp.pad(router_w, ((0, 0), (0, 0), (0, LANES - router_w.shape[-1])))
    return dict(w_main=w_main, w_ab=w_ab, w_branch=w_branch.astype(BF16), w_out=w_out.astype(BF16),
                ffn_w_in=ffn_w_in.astype(BF16), ffn_w_out=ffn_w_out.astype(BF16), w_r=w_r,
                moe_w_in=moe_w_in.astype(BF16), moe_w_out=moe_w_out.astype(BF16), depth=depth)


def _trunk(x3, mod, caches, pw, p, decode):
    n_seq, seq_len, d = x3.shape
    m = n_seq * seq_len
    x = x3.reshape(m, d)
    depth = pw["depth"]
    if decode:
        tm = tm_ffn = m
        tps = 1
    else:
        tm, tm_ffn = 512, 1024
        tps = seq_len // tm
    tps_ffn = seq_len // tm_ffn if not decode else 1

    def modv(l, idx, t):
        v = mod[l][:, idx]
        if decode:
            return jnp.repeat(v, seq_len, axis=0).reshape(m // t, t, d)
        return v.reshape(n_seq, 1, d)

    outs = ([], [], [], [], [])
    for l in range(depth):
        shift_m, scale_m, gate_m = (modv(l, i, tm) for i in range(3))
        zmain, zab = _in_proj(x, p["norm_g"][l, 0:1], modv(l, 1, tm_ffn), modv(l, 0, tm_ffn),
                              pw["w_main"][l], pw["w_ab"][l], tm_ffn, tps_ffn)
        if decode:
            cache_k, cache_v = caches[0][l], caches[1][l]
            nr = cache_k.shape[1]
            tab = _bias_table(p["rel_bias"][l], PAST_LEN, seq_len, PAST_LEN - nr, nr + seq_len)
            att = _attn_decode(zmain, cache_k.reshape(n_seq, nr, ATT_W), cache_v.reshape(n_seq, nr, ATT_W),
                               tab[:, :, :nr], tab[:, :, nr:], n_seq, seq_len)
            keep = seq_len
        else:
            qb, tq = 2 * CHUNK, N_BACK * CHUNK
            tab = _bias_table(p["rel_bias"][l], tq, qb, 0, tq + qb)
            att = _attn_prompt(zmain, tab, n_seq, seq_len)
            keep = min(N_BACK * CHUNK, seq_len)
        z3 = zmain.reshape(n_seq, seq_len, Z_W)
        k_rows = z3[:, seq_len - keep:, Z_K:Z_K + ATT_W].astype(F32).reshape(n_seq, keep, ATT_HEADS, ATT_HEAD_DIM)
        v_rows = z3[:, seq_len - keep:, Z_V:Z_V + ATT_W].astype(F32).reshape(n_seq, keep, ATT_HEADS, ATT_HEAD_DIM)
        if decode:
            cbuf = jnp.pad(caches[2][l], ((0, 0), (HALO - (CONV_WIDTH - 1), 0), (0, 0)))
            t_rows, sub = seq_len, seq_len
        else:
            cbuf = jnp.zeros((n_seq, HALO, CONV_CH), F32)
            t_rows, sub = 512, 64
        cw = jnp.pad(p["conv_w"][l], ((0, HALO - CONV_WIDTH), (0, 0)))
        cv, new_cbuf = _conv_module(zmain, cbuf, cw, p["conv_b"][l][None], p["conv_ln_g"][l][None],
                                    p["conv_ln_b"][l][None], n_seq, seq_len, t_rows, sub)
        if decode:
            dbuf = jnp.pad(caches[3][l], ((0, 0), (DN_HALO - (SHORT_CONV - 1), 0), (0, 0)))
            s0 = caches[4][l]
        else:
            dbuf = jnp.zeros((n_seq, DN_HALO, DN_CONV_CH), F32)
            s0 = jnp.zeros((n_seq, DN_HEADS, DN_DK, DN_DV), F32)
        cs = seq_len if seq_len <= CHUNK else CHUNK
        lane_pad = (0, LANES - DN_HEADS)
        expa = jnp.pad(jnp.exp(p["dn_A_log"][l]), lane_pad)[None]
        dtb = jnp.pad(p["dn_dt_bias"][l], lane_pad)[None]
        dn, new_dbuf, s_new = _deltanet(z3, zab.reshape(n_seq, seq_len, LANES), dbuf, s0, p["dn_conv_w"][l],
                                        expa, dtb, p["dn_norm_g"][l][None], cs, 4)
        x = _merge(att, cv, dn.reshape(m, DN_W), zmain, x, gate_m, pw["w_branch"][l], pw["w_out"][l], tm, tps)
        for lst, s in zip(outs, (k_rows, v_rows, new_cbuf, new_dbuf, s_new)):
            lst.append(s)
        shift_f, scale_f, gate_f = (modv(l, i, tm_ffn) for i in range(3, 6))
        final = l == depth - 1
        fg = p["final_norm_g"][None]
        if l % 2 == 0:
            x = _ffn(x, p["norm_g"][l, 1:2], scale_f, shift_f, gate_f, fg, pw["ffn_w_in"][l // 2],
                     pw["ffn_w_out"][l // 2], tm_ffn, tps_ffn, final)
        else:
            x = _moe(x, p["norm_g"][l, 1:2], scale_f, shift_f, gate_f, fg, pw["w_r"][l // 2], pw["moe_w_in"][l // 2],
                     pw["moe_w_out"][l // 2], tm_ffn, tps_ffn, final)
    return x.reshape(n_seq, seq_len, d), tuple(jnp.stack(s) for s in outs)


def kernel(x_prompt, x_sample, c_prompt, c_sample, cache_attn_k, cache_attn_v, state_conv, state_dn_conv, state_dn,
           w_ada, b_ada, norm_g, w_in, rel_bias, conv_w, conv_b, conv_ln_g, conv_ln_b, dn_conv_w, dn_A_log,
           dn_dt_bias, dn_norm_g, w_branch, w_out, ffn_w_in, ffn_w_out, router_w, moe_w_in, moe_w_out,
           final_norm_g):
    depth = w_in.shape[0]
    d = x_prompt.shape[-1]
    n_p = c_prompt.shape[0]
    p = dict(norm_g=norm_g, rel_bias=rel_bias, conv_w=conv_w, conv_b=conv_b, conv_ln_g=conv_ln_g,
             conv_ln_b=conv_ln_b, dn_conv_w=dn_conv_w, dn_A_log=dn_A_log, dn_dt_bias=dn_dt_bias,
             dn_norm_g=dn_norm_g, final_norm_g=final_norm_g)
    pw = _prep_weights(w_in, w_branch, w_out, ffn_w_in, ffn_w_out, router_w, moe_w_in, moe_w_out)
    mod = _ada(jnp.concatenate([c_prompt, c_sample], axis=0), w_ada, b_ada)
    mod = mod.reshape(depth, -1, 6, d)
    y_p, st_p = _trunk(x_prompt, mod[:, :n_p], None, pw, p, decode=False)
    y_s, st_s = _trunk(x_sample, mod[:, n_p:], (cache_attn_k, cache_attn_v, state_conv, state_dn_conv, state_dn),
                       pw, p, decode=True)
    return (y_p, y_s) + st_p + st_s
```

```python
import functools

import numpy as np
import jax
import jax.numpy as jnp
from jax import lax
from jax.experimental import pallas as pl
from jax.experimental.pallas import tpu as pltpu

F32 = jnp.float32
BF16 = jnp.bfloat16

PAST_LEN = 4096
CHUNK = 64
N_BACK = 8
ATT_HEADS = 8
ATT_HEAD_DIM = 64
ATT_W = ATT_HEADS * ATT_HEAD_DIM
REL_CLIP = 128
CONV_CH = 512
CONV_WIDTH = 31
DN_HEADS = 4
DN_DK = 128
DN_DV = 128
DN_QK_W = DN_HEADS * DN_DK
DN_W = DN_HEADS * DN_DV
DN_CONV_CH = 2 * DN_QK_W + DN_W
SHORT_CONV = 4
N_BRANCH = 3
BRANCH_W = 512
TOP_K = 2
NORM_EPS = 1e-6
NEG_BIG = -1e30

OFF_AQ = 0
OFF_GLU = 3 * ATT_W
OFF_DN_QKV = OFF_GLU + 2 * CONV_CH
OFF_DN_Z = OFF_DN_QKV + DN_CONV_CH
OFF_DN_A = OFF_DN_Z + DN_W
OFF_GATE = OFF_DN_A + 2 * DN_HEADS

Z_GATE = 0
Z_DNQKV = 3072
Z_Q = 4608
Z_K = 5120
Z_V = 5632
Z_U1 = 6144
Z_U2 = 6656
Z_DZ = 7168
Z_W = 7680
LANES = 128
HALO = 32
DN_HALO = 8

VMEM_LIMIT = 56 * 1024 * 1024


def _cparams(sem):
    return pltpu.CompilerParams(dimension_semantics=sem, vmem_limit_bytes=VMEM_LIMIT)


def _sigmoid(x):
    return 1.0 / (1.0 + jnp.exp(-x))


def _silu(x):
    return x * _sigmoid(x)


def _mod_norm(x, g, scale, shift):
    ms = jnp.mean(x * x, axis=-1, keepdims=True)
    return (x * lax.rsqrt(ms + NORM_EPS) * g) * (1.0 + scale) + shift


def _ada_kernel(c_ref, w_ref, b_ref, o_ref):
    s = _silu(c_ref[...])
    o_ref[0] = jnp.dot(s.astype(BF16), w_ref[0].astype(BF16), preferred_element_type=F32) + b_ref[0]


def _ada(c_all, w_ada, b_ada):
    depth, d, n6 = w_ada.shape
    n = c_all.shape[0]
    tn = 1024
    return pl.pallas_call(
        _ada_kernel,
        grid=(depth, n6 // tn),
        in_specs=[pl.BlockSpec((n, d), lambda l, j: (0, 0)),
                  pl.BlockSpec((1, d, tn), lambda l, j: (l, 0, j)),
                  pl.BlockSpec((1, 1, tn), lambda l, j: (l, 0, j))],
        out_specs=pl.BlockSpec((1, n, tn), lambda l, j: (l, 0, j)),
        out_shape=jax.ShapeDtypeStruct((depth, n, n6), F32),
        compiler_params=_cparams(("parallel", "parallel")),
        name="ada_mod",
    )(c_all, w_ada, b_ada.reshape(depth, 1, n6))


def _in_proj_kernel(x_ref, g_ref, sc_ref, sh_ref, w_ref, wab_ref, z_ref, zab_ref, h_scr):
    @pl.when(pl.program_id(1) == 0)
    def _():
        h = _mod_norm(x_ref[...], g_ref[...], sc_ref[0], sh_ref[0]).astype(BF16)
        h_scr[...] = h
        zab_ref[...] = jnp.dot(h, wab_ref[...], preferred_element_type=F32)

    z_ref[...] = jnp.dot(h_scr[...], w_ref[...], preferred_element_type=F32).astype(BF16)


def _mod_spec(mod, tm, tps):
    r = mod.shape[1]
    if r == 1:
        return pl.BlockSpec((1, 1, mod.shape[2]), lambda i, *_: (i // tps, 0, 0))
    return pl.BlockSpec((1, r, mod.shape[2]), lambda i, *_: (i, 0, 0))


def _in_proj(x, g, scale, shift, w_main, w_ab, tm, tps):
    m, d = x.shape
    tn = 512
    nt = Z_W // tn
    return pl.pallas_call(
        _in_proj_kernel,
        grid=(m // tm, nt),
        in_specs=[pl.BlockSpec((tm, d), lambda i, j: (i, 0)),
                  pl.BlockSpec((1, d), lambda i, j: (0, 0)),
                  _mod_spec(scale, tm, tps), _mod_spec(shift, tm, tps),
                  pl.BlockSpec((d, tn), lambda i, j: (0, j)),
                  pl.BlockSpec((d, LANES), lambda i, j: (0, 0))],
        out_specs=[pl.BlockSpec((tm, tn), lambda i, j: (i, j)),
                   pl.BlockSpec((tm, LANES), lambda i, j: (i, 0))],
        out_shape=[jax.ShapeDtypeStruct((m, Z_W), BF16), jax.ShapeDtypeStruct((m, LANES), F32)],
        scratch_shapes=[pltpu.VMEM((tm, d), BF16)],
        compiler_params=_cparams(("parallel", "arbitrary")),
        name="in_proj",
    )(x, g, scale, shift, w_main, w_ab)


def _softmax_pv(parts, vs):
    m = parts[0].max(axis=-1, keepdims=True)
    for s in parts[1:]:
        m = jnp.maximum(m, s.max(axis=-1, keepdims=True))
    num = None
    den = None
    for s, v in zip(parts, vs):
        p = jnp.exp(s - m)
        l = p.sum(axis=-1, keepdims=True)
        o = jnp.dot(p.astype(BF16), v, preferred_element_type=F32)
        num = o if num is None else num + o
        den = l if den is None else den + l
    return num / den


def _qk(q, k):
    return lax.dot_general(q, k, (((1,), (1,)), ((), ())), preferred_element_type=F32)


def _attn_prompt_kernel(q_ref, kp_ref, kc_ref, vp_ref, vc_ref, tab_ref, o_ref, *, tps, tq, qb, win):
    first = (pl.program_id(0) % tps == 0).astype(F32)
    kwin = jnp.concatenate([kp_ref[...], kc_ref[...]], axis=0)
    vwin = jnp.concatenate([vp_ref[...], vc_ref[...]], axis=0)
    rowid = lax.broadcasted_iota(jnp.int32, (1, 2 * tq), 1)
    neg = jnp.where(rowid < tq, first * NEG_BIG, 0.0)
    lo = lax.broadcasted_iota(jnp.int32, (qb, LANES), 1) < ATT_HEAD_DIM
    scale = ATT_HEAD_DIM ** -0.5
    for b in range(tq // qb):
        negw = neg[:, b * qb:b * qb + win]
        for hp in range(ATT_HEADS // 2):
            cols = slice(hp * LANES, (hp + 1) * LANES)
            q = q_ref[b * qb:(b + 1) * qb, cols]
            kw = kwin[b * qb:b * qb + win, cols]
            vw = vwin[b * qb:b * qb + win, cols]
            outs = []
            for half in range(2):
                qm = jnp.where(lo if half == 0 else jnp.logical_not(lo), q, jnp.zeros_like(q))
                s = _qk(qm, kw) * scale + tab_ref[2 * hp + half] + negw
                outs.append(_softmax_pv([s], [vw]))
            o_ref[b * qb:(b + 1) * qb, cols] = jnp.where(lo, outs[0], outs[1]).astype(BF16)


def _attn_prompt(zmain, tab, n_seq, seq_len):
    m = zmain.shape[0]
    tq = N_BACK * CHUNK
    qb = 2 * CHUNK
    win = tq + qb
    tps = seq_len // tq
    cq, ck, cv = Z_Q // ATT_W, Z_K // ATT_W, Z_V // ATT_W

    def prev(i):
        return jnp.where(i % tps == 0, i, i - 1)

    return pl.pallas_call(
        functools.partial(_attn_prompt_kernel, tps=tps, tq=tq, qb=qb, win=win),
        grid=(m // tq,),
        in_specs=[pl.BlockSpec((tq, ATT_W), lambda i: (i, cq)),
                  pl.BlockSpec((tq, ATT_W), lambda i: (prev(i), ck)),
                  pl.BlockSpec((tq, ATT_W), lambda i: (i, ck)),
                  pl.BlockSpec((tq, ATT_W), lambda i: (prev(i), cv)),
                  pl.BlockSpec((tq, ATT_W), lambda i: (i, cv)),
                  pl.BlockSpec((ATT_HEADS, qb, win), lambda i: (0, 0, 0))],
        out_specs=pl.BlockSpec((tq, ATT_W), lambda i: (i, 0)),
        out_shape=jax.ShapeDtypeStruct((m, ATT_W), BF16),
        compiler_params=_cparams(("parallel",)),
        name="attn_prompt",
    )(zmain, zmain, zmain, zmain, zmain, tab)


def _attn_decode_kernel(q_ref, kc_ref, kn_ref, vc_ref, vn_ref, tabc_ref, tabn_ref, o_ref):
    lq = q_ref.shape[0]
    kc = kc_ref[0].astype(BF16)
    vc = vc_ref[0].astype(BF16)
    lo = lax.broadcasted_iota(jnp.int32, (lq, LANES), 1) < ATT_HEAD_DIM
    scale = ATT_HEAD_DIM ** -0.5
    for hp in range(ATT_HEADS // 2):
        cols = slice(hp * LANES, (hp + 1) * LANES)
        q = q_ref[:, cols]
        outs = []
        for half in range(2):
            qm = jnp.where(lo if half == 0 else jnp.logical_not(lo), q, jnp.zeros_like(q))
            s1 = _qk(qm, kc[:, cols]) * scale + tabc_ref[2 * hp + half]
            s2 = _qk(qm, kn_ref[:, cols]) * scale + tabn_ref[2 * hp + half]
            outs.append(_softmax_pv([s1, s2], [vc[:, cols], vn_ref[:, cols]]))
        o_ref[:, cols] = jnp.where(lo, outs[0], outs[1]).astype(BF16)


def _attn_decode(zmain, cache_k, cache_v, tab_c, tab_n, n_seq, lq):
    m = zmain.shape[0]
    nr = cache_k.shape[1]
    cq, ck, cv = Z_Q // ATT_W, Z_K // ATT_W, Z_V // ATT_W
    return pl.pallas_call(
        _attn_decode_kernel,
        grid=(n_seq,),
        in_specs=[pl.BlockSpec((lq, ATT_W), lambda i: (i, cq)),
                  pl.BlockSpec((1, nr, ATT_W), lambda i: (i, 0, 0)),
                  pl.BlockSpec((lq, ATT_W), lambda i: (i, ck)),
                  pl.BlockSpec((1, nr, ATT_W), lambda i: (i, 0, 0)),
                  pl.BlockSpec((lq, ATT_W), lambda i: (i, cv)),
                  pl.BlockSpec((ATT_HEADS, lq, nr), lambda i: (0, 0, 0)),
                  pl.BlockSpec((ATT_HEADS, lq, lq), lambda i: (0, 0, 0))],
        out_specs=pl.BlockSpec((lq, ATT_W), lambda i: (i, 0)),
        out_shape=jax.ShapeDtypeStruct((m, ATT_W), BF16),
        compiler_params=_cparams(("parallel",)),
        name="attn_decode",
    )(zmain, cache_k, zmain, cache_v, zmain, tab_c, tab_n)


def _bias_table(rel_bias_l, qpos0, nq, kpos0, nk):
    qpos = qpos0 + np.arange(nq)
    kpos = kpos0 + np.arange(nk)
    qc = (qpos // CHUNK)[:, None]
    kc = (kpos // CHUNK)[None, :]
    ok = (kpos[None, :] >= 0) & (kc <= qc) & (kc >= qc - N_BACK)
    t_min = qpos0 - (kpos0 + nk - 1)
    t_max = qpos0 + nq - 1 - kpos0
    idx = np.clip(np.arange(t_max, t_min - 1, -1), -REL_CLIP, REL_CLIP) + REL_CLIP
    grev = rel_bias_l.astype(F32)[:, idx]
    rows = [grev[:, nq - 1 - i:nq - 1 - i + nk] for i in range(nq)]
    return jnp.where(ok[None], jnp.stack(rows, axis=1), NEG_BIG)


def _conv_kernel(u1_ref, u2_ref, buf_ref, w_ref, cb_ref, lg_ref, lb_ref, o_ref, nb_ref, xp_scr, *, t_rows, sub, n_t):
    t = pl.program_id(1)

    @pl.when(t == 0)
    def _():
        xp_scr[0:HALO, :] = buf_ref[0]

    u1 = u1_ref[...].astype(F32)
    u2 = u2_ref[...].astype(F32)
    xp_scr[HALO:HALO + t_rows, :] = u1 * _sigmoid(u2)
    off = HALO - (CONV_WIDTH - 1)
    for r0 in range(0, t_rows, sub):
        acc = w_ref[0:1, :] * xp_scr[r0 + off:r0 + off + sub, :]
        for j in range(1, CONV_WIDTH):
            acc = acc + w_ref[j:j + 1, :] * xp_scr[r0 + off + j:r0 + off + j + sub, :]
        cv = acc + cb_ref[...]
        mu = jnp.mean(cv, axis=-1, keepdims=True)
        cen = cv - mu
        var = jnp.mean(cen * cen, axis=-1, keepdims=True)
        y = cen * lax.rsqrt(var + NORM_EPS) * lg_ref[...] + lb_ref[...]
        o_ref[r0:r0 + sub, :] = _silu(y).astype(BF16)

    @pl.when(t == n_t - 1)
    def _():
        nb_ref[0] = xp_scr[t_rows + off:t_rows + HALO, :]

    xp_scr[0:HALO, :] = xp_scr[t_rows:t_rows + HALO, :]


def _conv_module(zmain, buf, w, cb, lg, lb, n_seq, seq_len, t_rows, sub):
    m = zmain.shape[0]
    n_t = seq_len // t_rows
    c1, c2 = Z_U1 // CONV_CH, Z_U2 // CONV_CH
    vec = pl.BlockSpec((1, CONV_CH), lambda n, t: (0, 0))
    return pl.pallas_call(
        functools.partial(_conv_kernel, t_rows=t_rows, sub=sub, n_t=n_t),
        grid=(n_seq, n_t),
        in_specs=[pl.BlockSpec((t_rows, CONV_CH), lambda n, t: (n * n_t + t, c1)),
                  pl.BlockSpec((t_rows, CONV_CH), lambda n, t: (n * n_t + t, c2)),
                  pl.BlockSpec((1, HALO, CONV_CH), lambda n, t: (n, 0, 0)),
                  pl.BlockSpec((HALO, CONV_CH), lambda n, t: (0, 0)),
                  vec, vec, vec],
        out_specs=[pl.BlockSpec((t_rows, CONV_CH), lambda n, t: (n * n_t + t, 0)),
                   pl.BlockSpec((1, CONV_WIDTH - 1, CONV_CH), lambda n, t: (n, 0, 0))],
        out_shape=[jax.ShapeDtypeStruct((m, CONV_CH), BF16),
                   jax.ShapeDtypeStruct((n_seq, CONV_WIDTH - 1, CONV_CH), F32)],
        scratch_shapes=[pltpu.VMEM((HALO + t_rows, CONV_CH), F32)],
        compiler_params=_cparams(("parallel", "arbitrary")),
        name="conv_module",
    )(zmain, zmain, buf, w, cb, lg, lb)


def _split3(x):
    hi = x.astype(BF16)
    r = x - hi.astype(F32)
    mid = r.astype(BF16)
    lo = (r - mid.astype(F32)).astype(BF16)
    return hi, mid, lo


def _dn_kernel(qkv_ref, dz_ref, ab_ref, buf_ref, s0_ref, cw_ref, expa_ref, dtb_ref, ng_ref,
               o_ref, nbuf_ref, sfin_ref, xp_scr, s_scr, *, nb, cs, n_c):
    c = pl.program_id(1)

    @pl.when(c == 0)
    def _():
        xp_scr[:, 0:DN_HALO, :] = buf_ref[...]
        s_scr[...] = s0_ref[...]

    lg = cs.bit_length() - 1
    wd = DN_HEADS * cs
    ii = lax.broadcasted_iota(jnp.int32, (cs, cs), 0)
    jj = lax.broadcasted_iota(jnp.int32, (cs, cs), 1)
    lincl = jnp.where(ii >= jj, 1.0, 0.0).astype(BF16)
    r = lax.broadcasted_iota(jnp.int32, (cs, wd), 0)
    col = lax.broadcasted_iota(jnp.int32, (cs, wd), 1)
    jl = col & (cs - 1)
    hid = col >> lg
    incl_s = r >= jl
    strict_s = r > jl
    diag_s = r == jl
    bdm = (lax.broadcasted_iota(jnp.int32, (wd, wd), 0) >> lg) == (lax.broadcasted_iota(jnp.int32, (wd, wd), 1) >> lg)
    kbm = (lax.broadcasted_iota(jnp.int32, (wd, DN_QK_W), 0) >> lg) == (
        lax.broadcasted_iota(jnp.int32, (wd, DN_QK_W), 1) // DN_DK)

    def bd(m):
        return jnp.where(bdm, jnp.concatenate([m] * DN_HEADS, axis=0), 0.0).astype(BF16)

    def level_sel(sh):
        bi = r >> sh
        return jnp.where((bi & 1) == 1, bi - 1, -1) == (jl >> sh)

    off = DN_HALO - (SHORT_CONV - 1)
    heads = range(DN_HEADS)
    for b in range(nb):
        xp_scr[b, DN_HALO:DN_HALO + cs, :] = qkv_ref[b].astype(F32)
        y = cw_ref[0:1, :] * xp_scr[b, off:off + cs, :]
        for j in range(1, SHORT_CONV):
            y = y + cw_ref[j:j + 1, :] * xp_scr[b, off + j:off + j + cs, :]
        y = _silu(y)
        ab = ab_ref[b]
        xg = ab + dtb_ref[...]
        softplus = jnp.maximum(xg, 0.0) + jnp.log(1.0 + jnp.exp(-jnp.abs(xg)))
        gfull = -expa_ref[...] * softplus
        beta_full = _sigmoid(ab)
        gc3 = jnp.dot(lincl, jnp.concatenate(_split3(gfull), axis=1), preferred_element_type=F32)
        gcum = gc3[:, :LANES] + gc3[:, LANES:2 * LANES] + gc3[:, 2 * LANES:]
        dz = dz_ref[b].astype(F32)
        q, k, kb, kbg, vb, qg, kd, gl = [], [], [], [], [], [], [], []
        gcx = None
        for h in heads:
            qh = y[:, h * DN_DK:(h + 1) * DN_DK]
            kh = y[:, DN_QK_W + h * DN_DK:DN_QK_W + (h + 1) * DN_DK]
            vh = y[:, 2 * DN_QK_W + h * DN_DV:2 * DN_QK_W + (h + 1) * DN_DV]
            qh = qh * (lax.rsqrt(jnp.sum(qh * qh, axis=-1, keepdims=True) + 1e-6) * (DN_DK ** -0.5))
            kh = kh * lax.rsqrt(jnp.sum(kh * kh, axis=-1, keepdims=True) + 1e-6)
            gc = gcum[:, h:h + 1]
            beta = beta_full[:, DN_HEADS + h:DN_HEADS + h + 1]
            glast = gcum[cs - 1:cs, h:h + 1]
            eg = jnp.exp(gc)
            gcx = gc if gcx is None else jnp.where(hid == h, gc, gcx)
            q.append(qh)
            k.append(kh)
            kb.append(kh * beta)
            kbg.append(kh * (beta * eg))
            vb.append(vh * beta)
            qg.append(qh * eg)
            kd.append(kh * jnp.exp(glast - gc))
            gl.append(jnp.exp(glast))
        gcx = jnp.broadcast_to(gcx, (cs, wd))
        grx = jnp.sum(jnp.where(diag_s, gcx, 0.0), axis=0, keepdims=True)
        decay = jnp.exp(jnp.where(incl_s, gcx - grx, NEG_BIG))
        x_all = jnp.concatenate([jnp.concatenate(kb, axis=1), jnp.concatenate(q, axis=1)], axis=0).astype(BF16)
        k_bd = jnp.where(kbm, jnp.concatenate([jnp.concatenate(k, axis=1)] * DN_HEADS, axis=0), 0.0).astype(BF16)
        kk = _qk(x_all, k_bd)
        a = jnp.where(strict_s, kk[:cs] * decay, 0.0)
        qk = kk[cs:] * decay
        d = jnp.where(diag_s, 1.0, 0.0) - jnp.where(level_sel(0), a, 0.0)
        for sh in range(1, lg):
            x = jnp.dot(d.astype(BF16), bd(jnp.where(level_sel(sh), a, 0.0)), preferred_element_type=F32)
            d = d - jnp.dot(x.astype(BF16), bd(d), preferred_element_type=F32)
        rhs = jnp.concatenate([jnp.concatenate([kbg[h], vb[h]], axis=1) for h in heads], axis=0).astype(BF16)
        wu = jnp.dot(bd(d), rhs, preferred_element_type=F32)
        s_old, o1, v_new = [], [], []
        for h in heads:
            s_old.append(s_scr[b, h])
            w_h = wu[h * cs:(h + 1) * cs, :DN_DK]
            wq = jnp.dot(jnp.concatenate([w_h, qg[h]], axis=0).astype(BF16), s_old[h].astype(BF16),
                         preferred_element_type=F32)
            v_new.append(wu[h * cs:(h + 1) * cs, DN_DK:] - wq[:cs])
            o1.append(wq[cs:])
        o2 = jnp.dot(bd(qk), jnp.concatenate(v_new, axis=0).astype(BF16), preferred_element_type=F32)
        for h in heads:
            o = o1[h] + o2[h * cs:(h + 1) * cs]
            s_scr[b, h] = s_old[h] * gl[h] + lax.dot_general(
                kd[h].astype(BF16), v_new[h].astype(BF16), (((0,), (0,)), ((), ())), preferred_element_type=F32)
            on = o * lax.rsqrt(jnp.mean(o * o, axis=-1, keepdims=True) + NORM_EPS) * ng_ref[...]
            dzh = dz[:, h * DN_DV:(h + 1) * DN_DV]
            o_ref[b, :, h * DN_DV:(h + 1) * DN_DV] = (on * _silu(dzh)).astype(BF16)

    @pl.when(c == n_c - 1)
    def _():
        nbuf_ref[...] = xp_scr[:, cs + off:cs + DN_HALO, :]
        sfin_ref[...] = s_scr[...]

    xp_scr[:, 0:DN_HALO, :] = xp_scr[:, cs:cs + DN_HALO, :]


def _deltanet(zmain3, zab3, buf, s0, cw, expa, dtb, ng, cs, nb):
    n_seq, seq_len, _ = zmain3.shape
    n_c = seq_len // cs
    cq, cz = Z_DNQKV // DN_CONV_CH, Z_DZ // DN_W
    row = pl.BlockSpec((1, LANES), lambda s, c: (0, 0))
    return pl.pallas_call(
        functools.partial(_dn_kernel, nb=nb, cs=cs, n_c=n_c),
        grid=(n_seq // nb, n_c),
        in_specs=[pl.BlockSpec((nb, cs, DN_CONV_CH), lambda s, c: (s, c, cq)),
                  pl.BlockSpec((nb, cs, DN_W), lambda s, c: (s, c, cz)),
                  pl.BlockSpec((nb, cs, LANES), lambda s, c: (s, c, 0)),
                  pl.BlockSpec((nb, DN_HALO, DN_CONV_CH), lambda s, c: (s, 0, 0)),
                  pl.BlockSpec((nb, DN_HEADS, DN_DK, DN_DV), lambda s, c: (s, 0, 0, 0)),
                  pl.BlockSpec((SHORT_CONV, DN_CONV_CH), lambda s, c: (0, 0)),
                  row, row, row],
        out_specs=[pl.BlockSpec((nb, cs, DN_W), lambda s, c: (s, c, 0)),
                   pl.BlockSpec((nb, SHORT_CONV - 1, DN_CONV_CH), lambda s, c: (s, 0, 0)),
                   pl.BlockSpec((nb, DN_HEADS, DN_DK, DN_DV), lambda s, c: (s, 0, 0, 0))],
        out_shape=[jax.ShapeDtypeStruct((n_seq, seq_len, DN_W), BF16),
                   jax.ShapeDtypeStruct((n_seq, SHORT_CONV - 1, DN_CONV_CH), F32),
                   jax.ShapeDtypeStruct((n_seq, DN_HEADS, DN_DK, DN_DV), F32)],
        scratch_shapes=[pltpu.VMEM((nb, DN_HALO + cs, DN_CONV_CH), F32),
                        pltpu.VMEM((nb, DN_HEADS, DN_DK, DN_DV), F32)],
        compiler_params=_cparams(("parallel", "arbitrary")),
        name="deltanet",
    )(zmain3, zmain3, zab3, buf, s0, cw, expa, dtb, ng)


def _merge_kernel(att_ref, cv_ref, dn_ref, g0_ref, g1_ref, g2_ref, x_ref, gm_ref, wb_ref, wo_ref, o_ref):
    acc = None
    for b, (br, gr) in enumerate(((att_ref, g0_ref), (cv_ref, g1_ref), (dn_ref, g2_ref))):
        yb = jnp.dot(br[...], wb_ref[b], preferred_element_type=F32)
        t = _sigmoid(gr[...].astype(F32)) * yb
        acc = t if acc is None else acc + t
    out = jnp.dot(acc.astype(BF16), wo_ref[...], preferred_element_type=F32)
    o_ref[...] = x_ref[...] + gm_ref[0] * out


def _merge(att, cv, dn, zmain, x, gate_m, wb, wo, tm, tps):
    m, d = x.shape
    br = pl.BlockSpec((tm, BRANCH_W), lambda i: (i, 0))
    return pl.pallas_call(
        _merge_kernel,
        grid=(m // tm,),
        in_specs=[br, br, br,
                  pl.BlockSpec((tm, d), lambda i: (i, 0)),
                  pl.BlockSpec((tm, d), lambda i: (i, 1)),
                  pl.BlockSpec((tm, d), lambda i: (i, 2)),
                  pl.BlockSpec((tm, d), lambda i: (i, 0)),
                  _mod_spec(gate_m, tm, tps),
                  pl.BlockSpec((N_BRANCH, BRANCH_W, d), lambda i: (0, 0, 0)),
                  pl.BlockSpec((d, d), lambda i: (0, 0))],
        out_specs=pl.BlockSpec((tm, d), lambda i: (i, 0)),
        out_shape=jax.ShapeDtypeStruct((m, d), F32),
        compiler_params=_cparams(("parallel",)),
        name="merge",
    )(att, cv, dn, zmain, zmain, zmain, x, gate_m, wb, wo)


def _finish(x_ref, gf_ref, fg_ref, acc, final):
    xn = x_ref[...] + gf_ref[0] * acc
    if final:
        ms = jnp.mean(xn * xn, axis=-1, keepdims=True)
        xn = xn * lax.rsqrt(ms + NORM_EPS) * fg_ref[...]
    return xn


def _ffn_kernel(x_ref, g_ref, sc_ref, sh_ref, gf_ref, fg_ref, wa_ref, wb_ref, wo_ref, o_ref, h_scr, acc_scr,
                *, n_f, final):
    j = pl.program_id(1)

    @pl.when(j == 0)
    def _():
        h_scr[...] = _mod_norm(x_ref[...], g_ref[...], sc_ref[0], sh_ref[0]).astype(BF16)
        acc_scr[...] = jnp.zeros_like(acc_scr)

    h = h_scr[...]
    a = jnp.dot(h, wa_ref[...], preferred_element_type=F32)
    b = jnp.dot(h, wb_ref[...], preferred_element_type=F32)
    acc_scr[...] += jnp.dot((_silu(a) * b).astype(BF16), wo_ref[...], preferred_element_type=F32)

    @pl.when(j == n_f - 1)
    def _():
        o_ref[...] = _finish(x_ref, gf_ref, fg_ref, acc_scr[...], final)


def _ffn(x, g, scale, shift, gate_f, fg, w_i, w_o, tm, tps, final):
    m, d = x.shape
    f = w_o.shape[0]
    tf = 256
    n_f = f // tf
    return pl.pallas_call(
        functools.partial(_ffn_kernel, n_f=n_f, final=final),
        grid=(m // tm, n_f),
        in_specs=[pl.BlockSpec((tm, d), lambda i, j: (i, 0)),
                  pl.BlockSpec((1, d), lambda i, j: (0, 0)),
                  _mod_spec(scale, tm, tps), _mod_spec(shift, tm, tps), _mod_spec(gate_f, tm, tps),
                  pl.BlockSpec((1, d), lambda i, j: (0, 0)),
                  pl.BlockSpec((d, tf), lambda i, j: (0, j)),
                  pl.BlockSpec((d, tf), lambda i, j: (0, j + n_f)),
                  pl.BlockSpec((tf, d), lambda i, j: (j, 0))],
        out_specs=pl.BlockSpec((tm, d), lambda i, j: (i, 0)),
        out_shape=jax.ShapeDtypeStruct((m, d), F32),
        scratch_shapes=[pltpu.VMEM((tm, d), BF16), pltpu.VMEM((tm, d), F32)],
        compiler_params=_cparams(("parallel", "arbitrary")),
        name="ffn",
    )(x, g, scale, shift, gate_f, fg, w_i, w_i, w_o)


MOE_TB = 512
MOE_GRAN = 16
MOE_TM = 512
MOE_RUN_BITS = 6


def _moe_local_rows(n_e):
    return TOP_K * MOE_TB + n_e * MOE_GRAN


def _run_copies(n, src_ref, src0, dst_ref, dst0, sem, wait):
    off = 0
    for bit in reversed(range(MOE_RUN_BITS)):
        size = MOE_GRAN << bit
        take = (n & size) != 0
        cp = pltpu.make_async_copy(src_ref.at[pl.ds(pl.multiple_of(src0 + off, MOE_GRAN), size)],
                                   dst_ref.at[pl.ds(pl.multiple_of(dst0 + off, MOE_GRAN), size)], sem)

        @pl.when(take)
        def _():
            if wait:
                cp.wait()
            else:
                cp.start()

        off = off + jnp.where(take, size, 0)


def _route_kernel(start_ref, x_ref, g_ref, sc_ref, sh_ref, wr_ref, *rest, n_e, n_b, aliased):
    if aliased:
        rest = rest[1:]
    route_ref, meta_ref, fill_ref, rows_ref, sorted_scr, cur_scr, sem = rest
    b = pl.program_id(0)
    tb = x_ref.shape[0]
    lr = sorted_scr.shape[0]

    @pl.when(b == 0)
    def _():
        for e in range(n_e):
            cur_scr[e] = start_ref[e]

    h = _mod_norm(x_ref[...], g_ref[...], sc_ref[0], sh_ref[0])
    logits = jnp.dot(h, wr_ref[...], preferred_element_type=F32, precision=lax.Precision.HIGHEST)
    lane = lax.broadcasted_iota(jnp.int32, (tb, LANES), 1)
    lg = jnp.where(lane < n_e, logits, NEG_BIG)
    m1 = lg.max(axis=-1, keepdims=True)
    i1 = jnp.where(lg == m1, lane, LANES).min(axis=-1, keepdims=True)
    lg2 = jnp.where(lane == i1, NEG_BIG, lg)
    m2 = lg2.max(axis=-1, keepdims=True)
    i2 = jnp.where(lg2 == m2, lane, LANES).min(axis=-1, keepdims=True)
    e2 = jnp.exp(m2 - m1)
    w1 = 1.0 / (1.0 + e2)
    w2 = e2 / (1.0 + e2)
    sel = jnp.where(lane == i1, 1.0, 0.0) + jnp.where(lane == i2, 1.0, 0.0)
    ti = lax.broadcasted_iota(jnp.int32, (tb, tb), 0)
    tj = lax.broadcasted_iota(jnp.int32, (tb, tb), 1)
    rank = jnp.dot(jnp.where(ti > tj, 1.0, 0.0).astype(BF16), sel.astype(BF16), preferred_element_type=F32)
    counts = jnp.sum(sel, axis=0, keepdims=True)
    cpad = jnp.floor((counts + (MOE_GRAN - 1)) * (1.0 / MOE_GRAN)) * MOE_GRAN
    ei = lax.broadcasted_iota(jnp.int32, (LANES, LANES), 0)
    ej = lax.broadcasted_iota(jnp.int32, (LANES, LANES), 1)
    loc = jnp.dot(jnp.broadcast_to(cpad, (8, LANES)).astype(BF16), jnp.where(ei < ej, 1.0, 0.0).astype(BF16),
                  preferred_element_type=F32)[0:1]
    pos = loc + rank
    p1 = jnp.sum(jnp.where(lane == i1, pos, 0.0), axis=-1, keepdims=True)
    p2 = jnp.sum(jnp.where(lane == i2, pos, 0.0), axis=-1, keepdims=True)
    route_ref[...] = jnp.where(lane == 0, p1, jnp.where(lane == 1, p2, jnp.where(lane == 2, w1,
                                                                                 jnp.where(lane == 3, w2, 0.0))))
    rr = lax.broadcasted_iota(jnp.int32, (tb, lr), 1)
    perm = jnp.where((rr == p1.astype(jnp.int32)) | (rr == p2.astype(jnp.int32)), 1.0, 0.0).astype(BF16)
    sorted_scr[...] = lax.dot_general(perm, h.astype(BF16), (((0,), (0,)), ((), ())),
                                      preferred_element_type=F32).astype(BF16)
    cp_i = cpad.astype(jnp.int32)
    loc_i = loc.astype(jnp.int32)
    runs = []
    for e in range(n_e):
        n = cp_i[0, e]
        dst0 = cur_scr[e]
        meta_ref[b, e] = dst0
        meta_ref[b, n_e + e] = n
        cur_scr[e] = dst0 + n
        runs.append((n, loc_i[0, e], dst0))
    for wait in (False, True):
        for e, (n, src0, dst0) in enumerate(runs):
            _run_copies(n, sorted_scr, src0, rows_ref.at[e], dst0, sem, wait)

    @pl.when(b == n_b - 1)
    def _():
        for e in range(n_e):
            fill_ref[e] = cur_scr[e]


def _route(x, g, scale, shift, w_r, start, rows_in, n_e, cap, tps):
    m, d = x.shape
    tb = MOE_TB
    n_b = m // tb
    lr = _moe_local_rows(n_e)
    aliased = rows_in is not None
    smem = pl.BlockSpec(memory_space=pltpu.SMEM)
    hbm = pl.BlockSpec(memory_space=pl.ANY)
    in_specs = [smem,
                pl.BlockSpec((tb, d), lambda i: (i, 0)),
                pl.BlockSpec((1, d), lambda i: (0, 0)),
                _mod_spec(scale, tb, tps), _mod_spec(shift, tb, tps),
                pl.BlockSpec((d, LANES), lambda i: (0, 0))]
    args = [start, x, g, scale, shift, w_r]
    if aliased:
        in_specs.append(hbm)
        args.append(rows_in)
    return pl.pallas_call(
        functools.partial(_route_kernel, n_e=n_e, n_b=n_b, aliased=aliased),
        grid=(n_b,),
        in_specs=in_specs,
        out_specs=[pl.BlockSpec((tb, LANES), lambda i: (i, 0)), smem, smem, hbm],
        out_shape=[jax.ShapeDtypeStruct((m, LANES), F32),
                   jax.ShapeDtypeStruct((n_b, 2 * n_e), jnp.int32),
                   jax.ShapeDtypeStruct((n_e,), jnp.int32),
                   jax.ShapeDtypeStruct((n_e, cap, d), BF16)],
        scratch_shapes=[pltpu.VMEM((lr, d), BF16), pltpu.SMEM((n_e,), jnp.int32), pltpu.SemaphoreType.DMA(())],
        input_output_aliases={len(args) - 1: 3} if aliased else {},
        compiler_params=_cparams(("arbitrary",)),
        name="moe_route",
    )(*args)


def _experts_kernel(te_ref, tr_ref, nv_ref, x_ref, wa_ref, wb_ref, wo_ref, y_ref, xs_scr, acc_scr, *, n_f):
    t = pl.program_id(0)
    j = pl.program_id(1)
    nv = nv_ref[t]

    @pl.when(nv > 0)
    def _():
        @pl.when(j == 0)
        def _():
            row = lax.broadcasted_iota(jnp.int32, xs_scr.shape, 0)
            xs_scr[...] = jnp.where(row < nv, x_ref[0].astype(F32), 0.0).astype(BF16)
            acc_scr[...] = jnp.zeros_like(acc_scr)

        xs = xs_scr[...]
        a = jnp.dot(xs, wa_ref[0], preferred_element_type=F32)
        b = jnp.dot(xs, wb_ref[0], preferred_element_type=F32)
        acc_scr[...] += jnp.dot((_silu(a) * b).astype(BF16), wo_ref[0], preferred_element_type=F32)

        @pl.when(j == n_f - 1)
        def _():
            y_ref[0] = acc_scr[...].astype(BF16)

    @pl.when((nv == 0) & (j == n_f - 1))
    def _():
        y_ref[0] = jnp.zeros(y_ref.shape[1:], BF16)


def _experts(rows, w_i, w_o, tile_e, tile_r, tile_nv):
    n_e, cap, d = rows.shape
    f = w_o.shape[1]
    tf = 512
    n_f = f // tf
    tm = MOE_TM
    n_t = tile_e.shape[0]

    def jv(t, j, nv):
        return jnp.where(nv[t] > 0, j, 0)

    return pl.pallas_call(
        functools.partial(_experts_kernel, n_f=n_f),
        grid_spec=pltpu.PrefetchScalarGridSpec(
            num_scalar_prefetch=3,
            grid=(n_t, n_f),
            in_specs=[pl.BlockSpec((1, tm, d), lambda t, j, te, tr, nv: (te[t], tr[t], 0)),
                      pl.BlockSpec((1, d, tf), lambda t, j, te, tr, nv: (te[t], 0, jv(t, j, nv))),
                      pl.BlockSpec((1, d, tf), lambda t, j, te, tr, nv: (te[t], 0, jv(t, j, nv) + n_f)),
                      pl.BlockSpec((1, tf, d), lambda t, j, te, tr, nv: (te[t], jv(t, j, nv), 0))],
            out_specs=pl.BlockSpec((1, tm, d), lambda t, j, te, tr, nv: (te[t], tr[t], 0)),
            scratch_shapes=[pltpu.VMEM((tm, d), BF16), pltpu.VMEM((tm, d), F32)]),
        out_shape=jax.ShapeDtypeStruct((n_e, cap, d), BF16),
        compiler_params=_cparams(("arbitrary", "arbitrary")),
        name="moe_experts",
    )(tile_e, tile_r, tile_nv, rows, w_i, w_i, w_o)


def _combine_kernel(meta_ref, y_ref, route_ref, x_ref, gf_ref, fg_ref, o_ref, yl_scr, sem, *, n_e, final):
    b = pl.program_id(0)
    tb = x_ref.shape[0]
    lr = yl_scr.shape[0]
    runs = []
    loc = 0
    for e in range(n_e):
        n = meta_ref[b, n_e + e]
        runs.append((n, meta_ref[b, e], loc))
        loc = loc + n
    for wait in (False, True):
        for e, (n, src0, dst0) in enumerate(runs):
            _run_copies(n, y_ref.at[e], src0, yl_scr, dst0, sem, wait)
    row = lax.broadcasted_iota(jnp.int32, yl_scr.shape, 0)
    yl = jnp.where(row < loc, yl_scr[...].astype(F32), 0.0).astype(BF16)
    route = route_ref[...]
    p1 = route[:, 0:1].astype(jnp.int32)
    p2 = route[:, 1:2].astype(jnp.int32)
    rr = lax.broadcasted_iota(jnp.int32, (tb, lr), 1)
    pw = jnp.where(rr == p1, route[:, 2:3], 0.0) + jnp.where(rr == p2, route[:, 3:4], 0.0)
    mix = jnp.dot(pw.astype(BF16), yl, preferred_element_type=F32)
    o_ref[...] = _finish(x_ref, gf_ref, fg_ref, mix, final)


def _combine(meta, y, route, x, gate_f, fg, n_e, tps, final):
    m, d = x.shape
    tb = MOE_TB
    lr = _moe_local_rows(n_e)
    return pl.pallas_call(
        functools.partial(_combine_kernel, n_e=n_e, final=final),
        grid_spec=pltpu.PrefetchScalarGridSpec(
            num_scalar_prefetch=1,
            grid=(m // tb,),
            in_specs=[pl.BlockSpec(memory_space=pl.ANY),
                      pl.BlockSpec((tb, LANES), lambda i, mt: (i, 0)),
                      pl.BlockSpec((tb, d), lambda i, mt: (i, 0)),
                      _mod_spec(gate_f, tb, tps),
                      pl.BlockSpec((1, d), lambda i, mt: (0, 0))],
            out_specs=pl.BlockSpec((tb, d), lambda i, mt: (i, 0)),
            scratch_shapes=[pltpu.VMEM((lr, d), BF16), pltpu.SemaphoreType.DMA(())]),
        out_shape=jax.ShapeDtypeStruct((m, d), F32),
        compiler_params=_cparams(("arbitrary",)),
        name="moe_combine",
    )(meta, y, route, x, gate_f, fg)


def _moe_sparse(streams, g, fg, w_r, w_i, w_o, final):
    n_e = w_o.shape[0]
    d = streams[0]["x"].shape[1]
    m_all = sum(s["x"].shape[0] for s in streams)
    n_b_all = m_all // MOE_TB
    worst_e = m_all + n_b_all * (MOE_GRAN - 1)
    worst = TOP_K * m_all + n_b_all * n_e * (MOE_GRAN - 1)
    cap = (-(-worst_e // MOE_TM) + 1) * MOE_TM
    n_t = -(-worst // MOE_TM) + n_e
    start = jnp.zeros((n_e,), jnp.int32)
    rows = None
    routed = []
    for s in streams:
        route, meta, start, rows = _route(s["x"], g, s["scale"], s["shift"], w_r, start, rows, n_e, cap, s["tps"])
        routed.append((route, meta))
    fill = start
    tiles_e = (fill + MOE_TM - 1) // MOE_TM
    first = jnp.cumsum(tiles_e) - tiles_e
    t = jnp.arange(n_t, dtype=jnp.int32)
    e_of = jnp.sum((t[:, None] >= (first + tiles_e)[None, :]).astype(jnp.int32), axis=1)
    valid = e_of < n_e
    e_cl = jnp.minimum(e_of, n_e - 1)
    r_of = t - first[e_cl]
    nv = jnp.where(valid, jnp.clip(fill[e_cl] - r_of * MOE_TM, 0, MOE_TM), 0).astype(jnp.int32)
    tile_e = jnp.where(valid, e_cl, n_e - 1).astype(jnp.int32)
    tile_r = jnp.where(valid, r_of, cap // MOE_TM - 1).astype(jnp.int32)
    y = _experts(rows, w_i, w_o, tile_e, tile_r, nv)
    return [_combine(meta, y, route, s["x"], s["gate_f"], fg, n_e, s["tps"], final)
            for s, (route, meta) in zip(streams, routed)]


def _prep_weights(w_in, w_branch, w_out, ffn_w_in, ffn_w_out, router_w, moe_w_in, moe_w_out):
    depth = w_in.shape[0]
    w_main = jnp.concatenate([w_in[..., OFF_GATE:], w_in[..., OFF_DN_QKV:OFF_DN_Z], w_in[..., OFF_AQ:OFF_GLU],
                              w_in[..., OFF_GLU:OFF_DN_QKV], w_in[..., OFF_DN_Z:OFF_DN_A]], axis=-1).astype(BF16)
    w_ab = jnp.pad(w_in[..., OFF_DN_A:OFF_GATE], ((0, 0), (0, 0), (0, LANES - 2 * DN_HEADS))).astype(BF16)
    w_r = jnp.pad(router_w, ((0, 0), (0, 0), (0, LANES - router_w.shape[-1])))
    return dict(w_main=w_main, w_ab=w_ab, w_branch=w_branch.astype(BF16), w_out=w_out.astype(BF16),
                ffn_w_in=ffn_w_in.astype(BF16), ffn_w_out=ffn_w_out.astype(BF16), w_r=w_r,
                moe_w_in=moe_w_in.astype(BF16), moe_w_out=moe_w_out.astype(BF16), depth=depth)


class _Stream:
    def __init__(self, x3, mod, caches):
        self.n_seq, self.seq_len, self.d = x3.shape
        self.m = self.n_seq * self.seq_len
        self.x = x3.reshape(self.m, self.d)
        self.mod = mod
        self.caches = caches
        self.decode = caches is not None
        self.states = ([], [], [], [], [])

    def tile(self, want):
        return self.m if self.decode else want

    def tps(self, t):
        return 1 if self.decode else self.seq_len // t

    def modv(self, l, idx, t):
        v = self.mod[l][:, idx]
        if self.decode:
            return jnp.repeat(v, self.seq_len, axis=0).reshape(self.m // t, t, self.d)
        return v.reshape(self.n_seq, 1, self.d)


def _mixers(s, l, pw, p):
    n_seq, seq_len, m, decode, caches = s.n_seq, s.seq_len, s.m, s.decode, s.caches
    x = s.x
    tm, tm_big = s.tile(512), s.tile(1024)
    zmain, zab = _in_proj(x, p["norm_g"][l, 0:1], s.modv(l, 1, tm_big), s.modv(l, 0, tm_big),
                          pw["w_main"][l], pw["w_ab"][l], tm_big, s.tps(tm_big))
    if decode:
        cache_k, cache_v = caches[0][l], caches[1][l]
        nr = cache_k.shape[1]
        tab = _bias_table(p["rel_bias"][l], PAST_LEN, seq_len, PAST_LEN - nr, nr + seq_len)
        att = _attn_decode(zmain, cache_k.reshape(n_seq, nr, ATT_W), cache_v.reshape(n_seq, nr, ATT_W),
                           tab[:, :, :nr], tab[:, :, nr:], n_seq, seq_len)
        keep = seq_len
    else:
        qb, tq = 2 * CHUNK, N_BACK * CHUNK
        tab = _bias_table(p["rel_bias"][l], tq, qb, 0, tq + qb)
        att = _attn_prompt(zmain, tab, n_seq, seq_len)
        keep = min(N_BACK * CHUNK, seq_len)
    z3 = zmain.reshape(n_seq, seq_len, Z_W)
    k_rows = z3[:, seq_len - keep:, Z_K:Z_K + ATT_W].astype(F32).reshape(n_seq, keep, ATT_HEADS, ATT_HEAD_DIM)
    v_rows = z3[:, seq_len - keep:, Z_V:Z_V + ATT_W].astype(F32).reshape(n_seq, keep, ATT_HEADS, ATT_HEAD_DIM)
    if decode:
        cbuf = jnp.pad(caches[2][l], ((0, 0), (HALO - (CONV_WIDTH - 1), 0), (0, 0)))
        t_rows, sub = seq_len, seq_len
    else:
        cbuf = jnp.zeros((n_seq, HALO, CONV_CH), F32)
        t_rows, sub = 512, 64
    cw = jnp.pad(p["conv_w"][l], ((0, HALO - CONV_WIDTH), (0, 0)))
    cv, new_cbuf = _conv_module(zmain, cbuf, cw, p["conv_b"][l][None], p["conv_ln_g"][l][None],
                                p["conv_ln_b"][l][None], n_seq, seq_len, t_rows, sub)
    if decode:
        dbuf = jnp.pad(caches[3][l], ((0, 0), (DN_HALO - (SHORT_CONV - 1), 0), (0, 0)))
        s0 = caches[4][l]
    else:
        dbuf = jnp.zeros((n_seq, DN_HALO, DN_CONV_CH), F32)
        s0 = jnp.zeros((n_seq, DN_HEADS, DN_DK, DN_DV), F32)
    cs = seq_len if seq_len <= CHUNK else CHUNK
    lane_pad = (0, LANES - DN_HEADS)
    expa = jnp.pad(jnp.exp(p["dn_A_log"][l]), lane_pad)[None]
    dtb = jnp.pad(p["dn_dt_bias"][l], lane_pad)[None]
    dn, new_dbuf, s_new = _deltanet(z3, zab.reshape(n_seq, seq_len, LANES), dbuf, s0, p["dn_conv_w"][l],
                                    expa, dtb, p["dn_norm_g"][l][None], cs, 4)
    s.x = _merge(att, cv, dn.reshape(m, DN_W), zmain, x, s.modv(l, 2, tm), pw["w_branch"][l], pw["w_out"][l],
                 tm, s.tps(tm))
    for lst, st in zip(s.states, (k_rows, v_rows, new_cbuf, new_dbuf, s_new)):
        lst.append(st)


def _channel_mixer(streams, l, pw, p, final):
    g = p["norm_g"][l, 1:2]
    fg = p["final_norm_g"][None]
    if l % 2 == 0:
        for s in streams:
            t = s.tile(1024)
            s.x = _ffn(s.x, g, s.modv(l, 4, t), s.modv(l, 3, t), s.modv(l, 5, t), fg, pw["ffn_w_in"][l // 2],
                       pw["ffn_w_out"][l // 2], t, s.tps(t), final)
    else:
        t = MOE_TB
        parts = [dict(x=s.x, scale=s.modv(l, 4, t), shift=s.modv(l, 3, t), gate_f=s.modv(l, 5, t),
                      tps=s.tps(t)) for s in streams]
        for s, xn in zip(streams, _moe_sparse(parts, g, fg, pw["w_r"][l // 2], pw["moe_w_in"][l // 2],
                                              pw["moe_w_out"][l // 2], final)):
            s.x = xn


def kernel(x_prompt, x_sample, c_prompt, c_sample, cache_attn_k, cache_attn_v, state_conv, state_dn_conv, state_dn,
           w_ada, b_ada, norm_g, w_in, rel_bias, conv_w, conv_b, conv_ln_g, conv_ln_b, dn_conv_w, dn_A_log,
           dn_dt_bias, dn_norm_g, w_branch, w_out, ffn_w_in, ffn_w_out, router_w, moe_w_in, moe_w_out,
           final_norm_g):
    depth = w_in.shape[0]
    d = x_prompt.shape[-1]
    n_p = c_prompt.shape[0]
    p = dict(norm_g=norm_g, rel_bias=rel_bias, conv_w=conv_w, conv_b=conv_b, conv_ln_g=conv_ln_g,
             conv_ln_b=conv_ln_b, dn_conv_w=dn_conv_w, dn_A_log=dn_A_log, dn_dt_bias=dn_dt_bias,
             dn_norm_g=dn_norm_g, final_norm_g=final_norm_g)
    pw = _prep_weights(w_in, w_branch, w_out, ffn_w_in, ffn_w_out, router_w, moe_w_in, moe_w_out)
    mod = _ada(jnp.concatenate([c_prompt, c_sample], axis=0), w_ada, b_ada)
    mod = mod.reshape(depth, -1, 6, d)
    streams = [_Stream(x_prompt, mod[:, :n_p], None),
               _Stream(x_sample, mod[:, n_p:], (cache_attn_k, cache_attn_v, state_conv, state_dn_conv, state_dn))]
    for l in range(depth):
        for s in streams:
            _mixers(s, l, pw, p)
        _channel_mixer(streams, l, pw, p, final=l == depth - 1)
    ys = tuple(s.x.reshape(s.n_seq, s.seq_len, d) for s in streams)
    return ys + tuple(jnp.stack(st) for s in streams for st in s.states)
```

```python
import functools

import numpy as np
import jax
import jax.numpy as jnp
from jax import lax
from jax.experimental import pallas as pl
from jax.experimental.pallas import tpu as pltpu

F32 = jnp.float32
BF16 = jnp.bfloat16

PAST_LEN = 4096
CHUNK = 64
N_BACK = 8
ATT_HEADS = 8
ATT_HEAD_DIM = 64
ATT_W = ATT_HEADS * ATT_HEAD_DIM
REL_CLIP = 128
CONV_CH = 512
CONV_WIDTH = 31
DN_HEADS = 4
DN_DK = 128
DN_DV = 128
DN_QK_W = DN_HEADS * DN_DK
DN_W = DN_HEADS * DN_DV
DN_CONV_CH = 2 * DN_QK_W + DN_W
SHORT_CONV = 4
N_BRANCH = 3
BRANCH_W = 512
TOP_K = 2
NORM_EPS = 1e-6
NEG_BIG = -1e30

OFF_AQ = 0
OFF_GLU = 3 * ATT_W
OFF_DN_QKV = OFF_GLU + 2 * CONV_CH
OFF_DN_Z = OFF_DN_QKV + DN_CONV_CH
OFF_DN_A = OFF_DN_Z + DN_W
OFF_GATE = OFF_DN_A + 2 * DN_HEADS

Z_GATE = 0
Z_DNQKV = 3072
Z_Q = 4608
Z_K = 5120
Z_V = 5632
Z_U1 = 6144
Z_U2 = 6656
Z_DZ = 7168
Z_W = 7680
LANES = 128
SUBLANES = 8
HALO = 32
DN_HALO = 8

VMEM_LIMIT = 56 * 1024 * 1024


def _cparams(sem):
    return pltpu.CompilerParams(dimension_semantics=sem, vmem_limit_bytes=VMEM_LIMIT)


def _sigmoid(x):
    return 1.0 / (1.0 + jnp.exp(-x))


def _silu(x):
    return x * _sigmoid(x)


def _mod_norm(x, g, scale, shift):
    ms = jnp.mean(x * x, axis=-1, keepdims=True)
    return (x * lax.rsqrt(ms + NORM_EPS) * g) * (1.0 + scale) + shift


def _ada_kernel(c_ref, w_ref, b_ref, o_ref):
    s = _silu(c_ref[...])
    o_ref[0] = jnp.dot(s.astype(BF16), w_ref[0].astype(BF16), preferred_element_type=F32) + b_ref[0]


def _ada(c_all, w_ada, b_ada):
    depth, d, n6 = w_ada.shape
    n = c_all.shape[0]
    tn = 1024
    return pl.pallas_call(
        _ada_kernel,
        grid=(depth, n6 // tn),
        in_specs=[pl.BlockSpec((n, d), lambda l, j: (0, 0)),
                  pl.BlockSpec((1, d, tn), lambda l, j: (l, 0, j)),
                  pl.BlockSpec((1, 1, tn), lambda l, j: (l, 0, j))],
        out_specs=pl.BlockSpec((1, n, tn), lambda l, j: (l, 0, j)),
        out_shape=jax.ShapeDtypeStruct((depth, n, n6), F32),
        compiler_params=_cparams(("parallel", "parallel")),
        name="ada_mod",
    )(c_all, w_ada, b_ada.reshape(depth, 1, n6))


def _in_proj_kernel(x_ref, g_ref, sc_ref, sh_ref, w_ref, wab_ref, z_ref, zab_ref, h_scr):
    @pl.when(pl.program_id(1) == 0)
    def _():
        h = _mod_norm(x_ref[...], g_ref[...], sc_ref[0], sh_ref[0]).astype(BF16)
        h_scr[...] = h
        zab_ref[...] = jnp.dot(h, wab_ref[...], preferred_element_type=F32)

    z_ref[...] = jnp.dot(h_scr[...], w_ref[...], preferred_element_type=F32).astype(BF16)


def _mod_spec(mod, tm, tps):
    r = mod.shape[1]
    if r == 1:
        return pl.BlockSpec((1, 1, mod.shape[2]), lambda i, *_: (i // tps, 0, 0))
    return pl.BlockSpec((1, r, mod.shape[2]), lambda i, *_: (i, 0, 0))


def _in_proj(x, g, scale, shift, w_main, w_ab, tm, tps):
    m, d = x.shape
    tn = 1536
    nt = Z_W // tn
    return pl.pallas_call(
        _in_proj_kernel,
        grid=(m // tm, nt),
        in_specs=[pl.BlockSpec((tm, d), lambda i, j: (i, 0)),
                  pl.BlockSpec((1, d), lambda i, j: (0, 0)),
                  _mod_spec(scale, tm, tps), _mod_spec(shift, tm, tps),
                  pl.BlockSpec((d, tn), lambda i, j: (0, j)),
                  pl.BlockSpec((d, LANES), lambda i, j: (0, 0))],
        out_specs=[pl.BlockSpec((tm, tn), lambda i, j: (i, j)),
                   pl.BlockSpec((tm, LANES), lambda i, j: (i, 0))],
        out_shape=[jax.ShapeDtypeStruct((m, Z_W), BF16), jax.ShapeDtypeStruct((m, LANES), F32)],
        scratch_shapes=[pltpu.VMEM((tm, d), BF16)],
        compiler_params=_cparams(("parallel", "arbitrary")),
        name="in_proj",
    )(x, g, scale, shift, w_main, w_ab)


def _softmax_pv(parts, vs):
    m = parts[0].max(axis=-1, keepdims=True)
    for s in parts[1:]:
        m = jnp.maximum(m, s.max(axis=-1, keepdims=True))
    num = None
    den = None
    for s, v in zip(parts, vs):
        p = jnp.exp(s - m)
        l = p.sum(axis=-1, keepdims=True)
        o = jnp.dot(p.astype(BF16), v, preferred_element_type=F32)
        num = o if num is None else num + o
        den = l if den is None else den + l
    return num / den


def _qk(q, k):
    return lax.dot_general(q, k, (((1,), (1,)), ((), ())), preferred_element_type=F32)


def _attn_prompt_kernel(q_ref, kp_ref, kc_ref, vp_ref, vc_ref, tab_ref, o_ref, *, tps, tq, qb, win):
    first = (pl.program_id(0) % tps == 0).astype(F32)
    kwin = jnp.concatenate([kp_ref[...], kc_ref[...]], axis=0)
    vwin = jnp.concatenate([vp_ref[...], vc_ref[...]], axis=0)
    rowid = lax.broadcasted_iota(jnp.int32, (1, 2 * tq), 1)
    neg = jnp.where(rowid < tq, first * NEG_BIG, 0.0)
    lo = lax.broadcasted_iota(jnp.int32, (qb, LANES), 1) < ATT_HEAD_DIM
    scale = ATT_HEAD_DIM ** -0.5
    for b in range(tq // qb):
        negw = neg[:, b * qb:b * qb + win]
        for hp in range(ATT_HEADS // 2):
            cols = slice(hp * LANES, (hp + 1) * LANES)
            q = q_ref[b * qb:(b + 1) * qb, cols] * scale
            kw = kwin[b * qb:b * qb + win, cols]
            vw = vwin[b * qb:b * qb + win, cols]
            zero = jnp.zeros_like(q)
            q2 = jnp.concatenate([jnp.where(lo, q, zero), jnp.where(lo, zero, q)], axis=0)
            o2 = _softmax_pv([_qk(q2, kw) + tab_ref[hp] + negw], [vw])
            o_ref[b * qb:(b + 1) * qb, cols] = jnp.where(lo, o2[:qb], o2[qb:]).astype(BF16)


def _attn_prompt(zmain, tab, n_seq, seq_len):
    m = zmain.shape[0]
    tq = N_BACK * CHUNK
    qb = 2 * CHUNK
    win = tq + qb
    tps = seq_len // tq
    cq, ck, cv = Z_Q // ATT_W, Z_K // ATT_W, Z_V // ATT_W

    def prev(i):
        return jnp.where(i % tps == 0, i, i - 1)

    return pl.pallas_call(
        functools.partial(_attn_prompt_kernel, tps=tps, tq=tq, qb=qb, win=win),
        grid=(m // tq,),
        in_specs=[pl.BlockSpec((tq, ATT_W), lambda i: (i, cq)),
                  pl.BlockSpec((tq, ATT_W), lambda i: (prev(i), ck)),
                  pl.BlockSpec((tq, ATT_W), lambda i: (i, ck)),
                  pl.BlockSpec((tq, ATT_W), lambda i: (prev(i), cv)),
                  pl.BlockSpec((tq, ATT_W), lambda i: (i, cv)),
                  pl.BlockSpec((ATT_HEADS // 2, 2 * qb, win), lambda i: (0, 0, 0))],
        out_specs=pl.BlockSpec((tq, ATT_W), lambda i: (i, 0)),
        out_shape=jax.ShapeDtypeStruct((m, ATT_W), BF16),
        compiler_params=_cparams(("parallel",)),
        name="attn_prompt",
    )(zmain, zmain, zmain, zmain, zmain, tab.reshape(ATT_HEADS // 2, 2 * qb, win))


def _attn_decode_kernel(q_ref, kc_ref, kn_ref, vc_ref, vn_ref, tabc_ref, tabn_ref, o_ref):
    lq = q_ref.shape[0]
    kc = kc_ref[0].astype(BF16)
    vc = vc_ref[0].astype(BF16)
    lo = lax.broadcasted_iota(jnp.int32, (lq, LANES), 1) < ATT_HEAD_DIM
    scale = ATT_HEAD_DIM ** -0.5
    for hp in range(ATT_HEADS // 2):
        cols = slice(hp * LANES, (hp + 1) * LANES)
        q = q_ref[:, cols]
        outs = []
        for half in range(2):
            qm = jnp.where(lo if half == 0 else jnp.logical_not(lo), q, jnp.zeros_like(q))
            s1 = _qk(qm, kc[:, cols]) * scale + tabc_ref[2 * hp + half]
            s2 = _qk(qm, kn_ref[:, cols]) * scale + tabn_ref[2 * hp + half]
            outs.append(_softmax_pv([s1, s2], [vc[:, cols], vn_ref[:, cols]]))
        o_ref[:, cols] = jnp.where(lo, outs[0], outs[1]).astype(BF16)


def _attn_decode(zmain, cache_k, cache_v, tab_c, tab_n, n_seq, lq):
    m = zmain.shape[0]
    nr = cache_k.shape[1]
    cq, ck, cv = Z_Q // ATT_W, Z_K // ATT_W, Z_V // ATT_W
    return pl.pallas_call(
        _attn_decode_kernel,
        grid=(n_seq,),
        in_specs=[pl.BlockSpec((lq, ATT_W), lambda i: (i, cq)),
                  pl.BlockSpec((1, nr, ATT_W), lambda i: (i, 0, 0)),
                  pl.BlockSpec((lq, ATT_W), lambda i: (i, ck)),
                  pl.BlockSpec((1, nr, ATT_W), lambda i: (i, 0, 0)),
                  pl.BlockSpec((lq, ATT_W), lambda i: (i, cv)),
                  pl.BlockSpec((ATT_HEADS, lq, nr), lambda i: (0, 0, 0)),
                  pl.BlockSpec((ATT_HEADS, lq, lq), lambda i: (0, 0, 0))],
        out_specs=pl.BlockSpec((lq, ATT_W), lambda i: (i, 0)),
        out_shape=jax.ShapeDtypeStruct((m, ATT_W), BF16),
        compiler_params=_cparams(("parallel",)),
        name="attn_decode",
    )(zmain, cache_k, zmain, cache_v, zmain, tab_c, tab_n)


def _bias_table(rel_bias_l, qpos0, nq, kpos0, nk):
    qpos = qpos0 + np.arange(nq)
    kpos = kpos0 + np.arange(nk)
    qc = (qpos // CHUNK)[:, None]
    kc = (kpos // CHUNK)[None, :]
    ok = (kpos[None, :] >= 0) & (kc <= qc) & (kc >= qc - N_BACK)
    t_min = qpos0 - (kpos0 + nk - 1)
    t_max = qpos0 + nq - 1 - kpos0
    idx = np.clip(np.arange(t_max, t_min - 1, -1), -REL_CLIP, REL_CLIP) + REL_CLIP
    grev = rel_bias_l.astype(F32)[:, idx]
    rows = [grev[:, nq - 1 - i:nq - 1 - i + nk] for i in range(nq)]
    return jnp.where(ok[None], jnp.stack(rows, axis=1), NEG_BIG)


def _conv_kernel(u1_ref, u2_ref, buf_ref, w_ref, cb_ref, lg_ref, lb_ref, o_ref, nb_ref, xp_scr, sh_scr,
                 *, t_rows, sub, n_t):
    t = pl.program_id(1)

    @pl.when(t == 0)
    def _():
        xp_scr[0:HALO, :] = buf_ref[0]

    u1 = u1_ref[...].astype(F32)
    u2 = u2_ref[...].astype(F32)
    xp_scr[HALO:HALO + t_rows, :] = u1 * _sigmoid(u2)
    off = HALO - (CONV_WIDTH - 1)
    n_sh = HALO + t_rows - SUBLANES
    for r in range(1, SUBLANES):
        sh_scr[r - 1, 0:n_sh, :] = xp_scr[r:r + n_sh, :]
    for r0 in range(0, t_rows, sub):
        acc = None
        for j in range(CONV_WIDTH):
            a, r = divmod(off + j, SUBLANES)
            src = xp_scr[r0 + a * SUBLANES:r0 + a * SUBLANES + sub, :] if r == 0 else \
                sh_scr[r - 1, r0 + a * SUBLANES:r0 + a * SUBLANES + sub, :]
            term = w_ref[j:j + 1, :] * src
            acc = term if acc is None else acc + term
        cv = acc + cb_ref[...]
        mu = jnp.mean(cv, axis=-1, keepdims=True)
        cen = cv - mu
        var = jnp.mean(cen * cen, axis=-1, keepdims=True)
        y = cen * lax.rsqrt(var + NORM_EPS) * lg_ref[...] + lb_ref[...]
        o_ref[r0:r0 + sub, :] = _silu(y).astype(BF16)

    @pl.when(t == n_t - 1)
    def _():
        nb_ref[0] = xp_scr[t_rows + off:t_rows + HALO, :]

    xp_scr[0:HALO, :] = xp_scr[t_rows:t_rows + HALO, :]


def _conv_module(zmain, buf, w, cb, lg, lb, n_seq, seq_len, t_rows, sub):
    m = zmain.shape[0]
    n_t = seq_len // t_rows
    c1, c2 = Z_U1 // CONV_CH, Z_U2 // CONV_CH
    vec = pl.BlockSpec((1, CONV_CH), lambda n, t: (0, 0))
    return pl.pallas_call(
        functools.partial(_conv_kernel, t_rows=t_rows, sub=sub, n_t=n_t),
        grid=(n_seq, n_t),
        in_specs=[pl.BlockSpec((t_rows, CONV_CH), lambda n, t: (n * n_t + t, c1)),
                  pl.BlockSpec((t_rows, CONV_CH), lambda n, t: (n * n_t + t, c2)),
                  pl.BlockSpec((1, HALO, CONV_CH), lambda n, t: (n, 0, 0)),
                  pl.BlockSpec((HALO, CONV_CH), lambda n, t: (0, 0)),
                  vec, vec, vec],
        out_specs=[pl.BlockSpec((t_rows, CONV_CH), lambda n, t: (n * n_t + t, 0)),
                   pl.BlockSpec((1, CONV_WIDTH - 1, CONV_CH), lambda n, t: (n, 0, 0))],
        out_shape=[jax.ShapeDtypeStruct((m, CONV_CH), BF16),
                   jax.ShapeDtypeStruct((n_seq, CONV_WIDTH - 1, CONV_CH), F32)],
        scratch_shapes=[pltpu.VMEM((HALO + t_rows, CONV_CH), F32),
                        pltpu.VMEM((SUBLANES - 1, HALO + t_rows - SUBLANES, CONV_CH), F32)],
        compiler_params=_cparams(("parallel", "arbitrary")),
        name="conv_module",
    )(zmain, zmain, buf, w, cb, lg, lb)


def _split3(x):
    hi = x.astype(BF16)
    r = x - hi.astype(F32)
    mid = r.astype(BF16)
    lo = (r - mid.astype(F32)).astype(BF16)
    return hi, mid, lo


def _dn_kernel(qkv_ref, dz_ref, ab_ref, buf_ref, s0_ref, cw_ref, expa_ref, dtb_ref, ng_ref,
               o_ref, nbuf_ref, sfin_ref, xp_scr, s_scr, *, nb, cs, n_c):
    c = pl.program_id(1)

    @pl.when(c == 0)
    def _():
        xp_scr[:, 0:DN_HALO, :] = buf_ref[...]
        s_scr[...] = s0_ref[...]

    lg = cs.bit_length() - 1
    wd = DN_HEADS * cs
    ii = lax.broadcasted_iota(jnp.int32, (cs, cs), 0)
    jj = lax.broadcasted_iota(jnp.int32, (cs, cs), 1)
    lincl = jnp.where(ii >= jj, 1.0, 0.0).astype(BF16)
    r = lax.broadcasted_iota(jnp.int32, (cs, wd), 0)
    col = lax.broadcasted_iota(jnp.int32, (cs, wd), 1)
    jl = col & (cs - 1)
    hid = col >> lg
    incl_s = r >= jl
    strict_s = r > jl
    diag_s = r == jl
    rb = lax.broadcasted_iota(jnp.int32, (wd, wd), 0)
    cb = lax.broadcasted_iota(jnp.int32, (wd, wd), 1)
    same_head = (rb >> lg) == (cb >> lg)
    bd_mask = jnp.where(same_head, 1.0, 0.0).astype(BF16)
    kb_mask = jnp.where((lax.broadcasted_iota(jnp.int32, (wd, DN_QK_W), 0) >> lg) == (
        lax.broadcasted_iota(jnp.int32, (wd, DN_QK_W), 1) // DN_DK), 1.0, 0.0).astype(BF16)

    def level_sel(row, colm, sh):
        bi = row >> sh
        return jnp.where((bi & 1) == 1, bi - 1, -1) == (colm >> sh)

    level_mask = [jnp.where(same_head & level_sel(rb & (cs - 1), cb & (cs - 1), sh), 1.0, 0.0).astype(BF16)
                  for sh in range(1, lg)]

    def bd(m):
        return jnp.concatenate([m.astype(BF16)] * DN_HEADS, axis=0) * bd_mask

    off = DN_HALO - (SHORT_CONV - 1)
    heads = range(DN_HEADS)
    a_bd, d, qk, rhs, qg, kd, gl, dz = [], [], [], [], [], [], [], []
    for b in range(nb):
        xp_scr[b, DN_HALO:DN_HALO + cs, :] = qkv_ref[b].astype(F32)
        y = cw_ref[0:1, :] * xp_scr[b, off:off + cs, :]
        for j in range(1, SHORT_CONV):
            y = y + cw_ref[j:j + 1, :] * xp_scr[b, off + j:off + j + cs, :]
        y = _silu(y)
        ab = ab_ref[b]
        xg = ab + dtb_ref[...]
        softplus = jnp.maximum(xg, 0.0) + jnp.log(1.0 + jnp.exp(-jnp.abs(xg)))
        gfull = -expa_ref[...] * softplus
        beta_full = _sigmoid(ab)
        gc3 = jnp.dot(lincl, jnp.concatenate(_split3(gfull), axis=1), preferred_element_type=F32)
        gcum = gc3[:, :LANES] + gc3[:, LANES:2 * LANES] + gc3[:, 2 * LANES:]
        dz.append(dz_ref[b].astype(F32))
        q, k, kb, kbg, vb, qg_b, kd_b, gl_b = [], [], [], [], [], [], [], []
        gcx = None
        for h in heads:
            qh = y[:, h * DN_DK:(h + 1) * DN_DK]
            kh = y[:, DN_QK_W + h * DN_DK:DN_QK_W + (h + 1) * DN_DK]
            vh = y[:, 2 * DN_QK_W + h * DN_DV:2 * DN_QK_W + (h + 1) * DN_DV]
            qh = qh * (lax.rsqrt(jnp.sum(qh * qh, axis=-1, keepdims=True) + 1e-6) * (DN_DK ** -0.5))
            kh = kh * lax.rsqrt(jnp.sum(kh * kh, axis=-1, keepdims=True) + 1e-6)
            gc = gcum[:, h:h + 1]
            beta = beta_full[:, DN_HEADS + h:DN_HEADS + h + 1]
            glast = gcum[cs - 1:cs, h:h + 1]
            eg = jnp.exp(gc)
            gcx = gc if gcx is None else jnp.where(hid == h, gc, gcx)
            q.append(qh)
            k.append(kh)
            kb.append(kh * beta)
            kbg.append(kh * (beta * eg))
            vb.append(vh * beta)
            qg_b.append(qh * eg)
            kd_b.append(kh * jnp.exp(glast - gc))
            gl_b.append(jnp.exp(glast))
        gcx = jnp.broadcast_to(gcx, (cs, wd))
        grx = jnp.sum(jnp.where(diag_s, gcx, 0.0), axis=0, keepdims=True)
        decay = jnp.exp(jnp.where(incl_s, gcx - grx, NEG_BIG))
        x_all = jnp.concatenate([jnp.concatenate(kb, axis=1), jnp.concatenate(q, axis=1)], axis=0).astype(BF16)
        k_bd = jnp.concatenate([jnp.concatenate(k, axis=1).astype(BF16)] * DN_HEADS, axis=0) * kb_mask
        kk = _qk(x_all, k_bd)
        a = jnp.where(strict_s, kk[:cs] * decay, 0.0)
        a_bd.append(bd(a))
        d.append(jnp.where(diag_s, 1.0, 0.0) - jnp.where(level_sel(r, jl, 0), a, 0.0))
        qk.append(kk[cs:] * decay)
        rhs.append(jnp.concatenate([jnp.concatenate([kbg[h], vb[h]], axis=1) for h in heads],
                                   axis=0).astype(BF16))
        qg.append(qg_b)
        kd.append(kd_b)
        gl.append(gl_b)
    for lm in level_mask:
        for b in range(nb):
            x = jnp.dot(d[b].astype(BF16), a_bd[b] * lm, preferred_element_type=F32)
            d[b] = d[b] - jnp.dot(x.astype(BF16), bd(d[b]), preferred_element_type=F32)
    wu = [jnp.dot(bd(d[b]), rhs[b], preferred_element_type=F32) for b in range(nb)]
    s_old = [[s_scr[b, h] for h in heads] for b in range(nb)]
    wq = [[jnp.dot(jnp.concatenate([wu[b][h * cs:(h + 1) * cs, :DN_DK], qg[b][h]], axis=0).astype(BF16),
                   s_old[b][h].astype(BF16), preferred_element_type=F32) for h in heads] for b in range(nb)]
    v_new = [[wu[b][h * cs:(h + 1) * cs, DN_DK:] - wq[b][h][:cs] for h in heads] for b in range(nb)]
    o2 = [jnp.dot(bd(qk[b]), jnp.concatenate(v_new[b], axis=0).astype(BF16), preferred_element_type=F32)
          for b in range(nb)]
    for b in range(nb):
        for h in heads:
            o = wq[b][h][cs:] + o2[b][h * cs:(h + 1) * cs]
            s_scr[b, h] = s_old[b][h] * gl[b][h] + lax.dot_general(
                kd[b][h].astype(BF16), v_new[b][h].astype(BF16), (((0,), (0,)), ((), ())),
                preferred_element_type=F32)
            on = o * lax.rsqrt(jnp.mean(o * o, axis=-1, keepdims=True) + NORM_EPS) * ng_ref[...]
            dzh = dz[b][:, h * DN_DV:(h + 1) * DN_DV]
            o_ref[b, :, h * DN_DV:(h + 1) * DN_DV] = (on * _silu(dzh)).astype(BF16)

    @pl.when(c == n_c - 1)
    def _():
        nbuf_ref[...] = xp_scr[:, cs + off:cs + DN_HALO, :]
        sfin_ref[...] = s_scr[...]

    xp_scr[:, 0:DN_HALO, :] = xp_scr[:, cs:cs + DN_HALO, :]


def _deltanet(zmain3, zab3, buf, s0, cw, expa, dtb, ng, cs, nb):
    n_seq, seq_len, _ = zmain3.shape
    n_c = seq_len // cs
    cq, cz = Z_DNQKV // DN_CONV_CH, Z_DZ // DN_W
    row = pl.BlockSpec((1, LANES), lambda s, c: (0, 0))
    return pl.pallas_call(
        functools.partial(_dn_kernel, nb=nb, cs=cs, n_c=n_c),
        grid=(n_seq // nb, n_c),
        in_specs=[pl.BlockSpec((nb, cs, DN_CONV_CH), lambda s, c: (s, c, cq)),
                  pl.BlockSpec((nb, cs, DN_W), lambda s, c: (s, c, cz)),
                  pl.BlockSpec((nb, cs, LANES), lambda s, c: (s, c, 0)),
                  pl.BlockSpec((nb, DN_HALO, DN_CONV_CH), lambda s, c: (s, 0, 0)),
                  pl.BlockSpec((nb, DN_HEADS, DN_DK, DN_DV), lambda s, c: (s, 0, 0, 0)),
                  pl.BlockSpec((SHORT_CONV, DN_CONV_CH), lambda s, c: (0, 0)),
                  row, row, row],
        out_specs=[pl.BlockSpec((nb, cs, DN_W), lambda s, c: (s, c, 0)),
                   pl.BlockSpec((nb, SHORT_CONV - 1, DN_CONV_CH), lambda s, c: (s, 0, 0)),
                   pl.BlockSpec((nb, DN_HEADS, DN_DK, DN_DV), lambda s, c: (s, 0, 0, 0))],
        out_shape=[jax.ShapeDtypeStruct((n_seq, seq_len, DN_W), BF16),
                   jax.ShapeDtypeStruct((n_seq, SHORT_CONV - 1, DN_CONV_CH), F32),
                   jax.ShapeDtypeStruct((n_seq, DN_HEADS, DN_DK, DN_DV), F32)],
        scratch_shapes=[pltpu.VMEM((nb, DN_HALO + cs, DN_CONV_CH), F32),
                        pltpu.VMEM((nb, DN_HEADS, DN_DK, DN_DV), F32)],
        compiler_params=_cparams(("parallel", "arbitrary")),
        name="deltanet",
    )(zmain3, zmain3, zab3, buf, s0, cw, expa, dtb, ng)


def _merge_kernel(att_ref, cv_ref, dn_ref, g0_ref, g1_ref, g2_ref, x_ref, gm_ref, wb_ref, wo_ref, o_ref):
    acc = None
    for b, (br, gr) in enumerate(((att_ref, g0_ref), (cv_ref, g1_ref), (dn_ref, g2_ref))):
        yb = jnp.dot(br[...], wb_ref[b], preferred_element_type=F32)
        t = _sigmoid(gr[...].astype(F32)) * yb
        acc = t if acc is None else acc + t
    out = jnp.dot(acc.astype(BF16), wo_ref[...], preferred_element_type=F32)
    o_ref[...] = x_ref[...] + gm_ref[0] * out


def _merge(att, cv, dn, zmain, x, gate_m, wb, wo, tm, tps):
    m, d = x.shape
    br = pl.BlockSpec((tm, BRANCH_W), lambda i: (i, 0))
    return pl.pallas_call(
        _merge_kernel,
        grid=(m // tm,),
        in_specs=[br, br, br,
                  pl.BlockSpec((tm, d), lambda i: (i, 0)),
                  pl.BlockSpec((tm, d), lambda i: (i, 1)),
                  pl.BlockSpec((tm, d), lambda i: (i, 2)),
                  pl.BlockSpec((tm, d), lambda i: (i, 0)),
                  _mod_spec(gate_m, tm, tps),
                  pl.BlockSpec((N_BRANCH, BRANCH_W, d), lambda i: (0, 0, 0)),
                  pl.BlockSpec((d, d), lambda i: (0, 0))],
        out_specs=pl.BlockSpec((tm, d), lambda i: (i, 0)),
        out_shape=jax.ShapeDtypeStruct((m, d), F32),
        compiler_params=_cparams(("parallel",)),
        name="merge",
    )(att, cv, dn, zmain, zmain, zmain, x, gate_m, wb, wo)


def _finish(x_ref, gf_ref, fg_ref, acc, final):
    xn = x_ref[...] + gf_ref[0] * acc
    if final:
        ms = jnp.mean(xn * xn, axis=-1, keepdims=True)
        xn = xn * lax.rsqrt(ms + NORM_EPS) * fg_ref[...]
    return xn


def _ffn_kernel(x_ref, g_ref, sc_ref, sh_ref, gf_ref, fg_ref, wa_ref, wb_ref, wo_ref, o_ref, h_scr, acc_scr,
                *, n_f, final):
    j = pl.program_id(1)

    @pl.when(j == 0)
    def _():
        h_scr[...] = _mod_norm(x_ref[...], g_ref[...], sc_ref[0], sh_ref[0]).astype(BF16)
        acc_scr[...] = jnp.zeros_like(acc_scr)

    h = h_scr[...]
    a = jnp.dot(h, wa_ref[...], preferred_element_type=F32)
    b = jnp.dot(h, wb_ref[...], preferred_element_type=F32)
    acc_scr[...] += jnp.dot((_silu(a) * b).astype(BF16), wo_ref[...], preferred_element_type=F32)

    @pl.when(j == n_f - 1)
    def _():
        o_ref[...] = _finish(x_ref, gf_ref, fg_ref, acc_scr[...], final)


def _hidden_tile(f, limit):
    return max(t for t in range(LANES, min(f, limit) + 1, LANES) if f % t == 0)


def _ffn(x, g, scale, shift, gate_f, fg, w_i, w_o, tm, tps, final):
    m, d = x.shape
    f = w_o.shape[0]
    tf = _hidden_tile(f, 1408)
    n_f = f // tf
    return pl.pallas_call(
        functools.partial(_ffn_kernel, n_f=n_f, final=final),
        grid=(m // tm, n_f),
        in_specs=[pl.BlockSpec((tm, d), lambda i, j: (i, 0)),
                  pl.BlockSpec((1, d), lambda i, j: (0, 0)),
                  _mod_spec(scale, tm, tps), _mod_spec(shift, tm, tps), _mod_spec(gate_f, tm, tps),
                  pl.BlockSpec((1, d), lambda i, j: (0, 0)),
                  pl.BlockSpec((d, tf), lambda i, j: (0, j)),
                  pl.BlockSpec((d, tf), lambda i, j: (0, j + n_f)),
                  pl.BlockSpec((tf, d), lambda i, j: (j, 0))],
        out_specs=pl.BlockSpec((tm, d), lambda i, j: (i, 0)),
        out_shape=jax.ShapeDtypeStruct((m, d), F32),
        scratch_shapes=[pltpu.VMEM((tm, d), BF16), pltpu.VMEM((tm, d), F32)],
        compiler_params=_cparams(("parallel", "arbitrary")),
        name="ffn",
    )(x, g, scale, shift, gate_f, fg, w_i, w_i, w_o)


MOE_TB = 512
MOE_GRAN = 16
MOE_TM = 512
MOE_RUN_BITS = 6


def _moe_local_rows(n_e):
    return TOP_K * MOE_TB + n_e * MOE_GRAN


def _run_copies(n, src_ref, src0, dst_ref, dst0, sem, wait):
    off = 0
    for bit in reversed(range(MOE_RUN_BITS)):
        size = MOE_GRAN << bit
        take = (n & size) != 0
        cp = pltpu.make_async_copy(src_ref.at[pl.ds(pl.multiple_of(src0 + off, MOE_GRAN), size)],
                                   dst_ref.at[pl.ds(pl.multiple_of(dst0 + off, MOE_GRAN), size)], sem)

        @pl.when(take)
        def _():
            if wait:
                cp.wait()
            else:
                cp.start()

        off = off + jnp.where(take, size, 0)


def _route_kernel(start_ref, x_ref, g_ref, sc_ref, sh_ref, wr_ref, *rest, n_e, n_b, aliased):
    if aliased:
        rest = rest[1:]
    route_ref, meta_ref, fill_ref, rows_ref, sorted_scr, cur_scr, sem = rest
    b = pl.program_id(0)
    tb = x_ref.shape[0]
    lr = sorted_scr.shape[0]

    @pl.when(b == 0)
    def _():
        for e in range(n_e):
            cur_scr[e] = start_ref[e]

    h = _mod_norm(x_ref[...], g_ref[...], sc_ref[0], sh_ref[0])
    logits = jnp.dot(h, wr_ref[...], preferred_element_type=F32, precision=lax.Precision.HIGHEST)
    lane = lax.broadcasted_iota(jnp.int32, (tb, LANES), 1)
    lg = jnp.where(lane < n_e, logits, NEG_BIG)
    m1 = lg.max(axis=-1, keepdims=True)
    i1 = jnp.where(lg == m1, lane, LANES).min(axis=-1, keepdims=True)
    lg2 = jnp.where(lane == i1, NEG_BIG, lg)
    m2 = lg2.max(axis=-1, keepdims=True)
    i2 = jnp.where(lg2 == m2, lane, LANES).min(axis=-1, keepdims=True)
    e2 = jnp.exp(m2 - m1)
    w1 = 1.0 / (1.0 + e2)
    w2 = e2 / (1.0 + e2)
    sel = jnp.where(lane == i1, 1.0, 0.0) + jnp.where(lane == i2, 1.0, 0.0)
    ti = lax.broadcasted_iota(jnp.int32, (tb, tb), 0)
    tj = lax.broadcasted_iota(jnp.int32, (tb, tb), 1)
    rank = jnp.dot(jnp.where(ti > tj, 1.0, 0.0).astype(BF16), sel.astype(BF16), preferred_element_type=F32)
    counts = jnp.sum(sel, axis=0, keepdims=True)
    cpad = jnp.floor((counts + (MOE_GRAN - 1)) * (1.0 / MOE_GRAN)) * MOE_GRAN
    ei = lax.broadcasted_iota(jnp.int32, (LANES, LANES), 0)
    ej = lax.broadcasted_iota(jnp.int32, (LANES, LANES), 1)
    loc = jnp.dot(jnp.broadcast_to(cpad, (8, LANES)).astype(BF16), jnp.where(ei < ej, 1.0, 0.0).astype(BF16),
                  preferred_element_type=F32)[0:1]
    pos = loc + rank
    p1 = jnp.sum(jnp.where(lane == i1, pos, 0.0), axis=-1, keepdims=True)
    p2 = jnp.sum(jnp.where(lane == i2, pos, 0.0), axis=-1, keepdims=True)
    route_ref[...] = jnp.where(lane == 0, p1, jnp.where(lane == 1, p2, jnp.where(lane == 2, w1,
                                                                                 jnp.where(lane == 3, w2, 0.0))))
    rr = lax.broadcasted_iota(jnp.int32, (tb, lr), 1)
    perm = jnp.where((rr == p1.astype(jnp.int32)) | (rr == p2.astype(jnp.int32)), 1.0, 0.0).astype(BF16)
    sorted_scr[...] = lax.dot_general(perm, h.astype(BF16), (((0,), (0,)), ((), ())),
                                      preferred_element_type=F32).astype(BF16)
    cp_i = cpad.astype(jnp.int32)
    loc_i = loc.astype(jnp.int32)
    runs = []
    for e in range(n_e):
        n = cp_i[0, e]
        dst0 = cur_scr[e]
        meta_ref[b, e] = dst0
        meta_ref[b, n_e + e] = n
        cur_scr[e] = dst0 + n
        runs.append((n, loc_i[0, e], dst0))
    for wait in (False, True):
        for e, (n, src0, dst0) in enumerate(runs):
            _run_copies(n, sorted_scr, src0, rows_ref.at[e], dst0, sem, wait)

    @pl.when(b == n_b - 1)
    def _():
        for e in range(n_e):
            fill_ref[e] = cur_scr[e]


def _route(x, g, scale, shift, w_r, start, rows_in, n_e, cap, tps):
    m, d = x.shape
    tb = MOE_TB
    n_b = m // tb
    lr = _moe_local_rows(n_e)
    aliased = rows_in is not None
    smem = pl.BlockSpec(memory_space=pltpu.SMEM)
    hbm = pl.BlockSpec(memory_space=pl.ANY)
    in_specs = [smem,
                pl.BlockSpec((tb, d), lambda i: (i, 0)),
                pl.BlockSpec((1, d), lambda i: (0, 0)),
                _mod_spec(scale, tb, tps), _mod_spec(shift, tb, tps),
                pl.BlockSpec((d, LANES), lambda i: (0, 0))]
    args = [start, x, g, scale, shift, w_r]
    if aliased:
        in_specs.append(hbm)
        args.append(rows_in)
    return pl.pallas_call(
        functools.partial(_route_kernel, n_e=n_e, n_b=n_b, aliased=aliased),
        grid=(n_b,),
        in_specs=in_specs,
        out_specs=[pl.BlockSpec((tb, LANES), lambda i: (i, 0)), smem, smem, hbm],
        out_shape=[jax.ShapeDtypeStruct((m, LANES), F32),
                   jax.ShapeDtypeStruct((n_b, 2 * n_e), jnp.int32),
                   jax.ShapeDtypeStruct((n_e,), jnp.int32),
                   jax.ShapeDtypeStruct((n_e, cap, d), BF16)],
        scratch_shapes=[pltpu.VMEM((lr, d), BF16), pltpu.SMEM((n_e,), jnp.int32), pltpu.SemaphoreType.DMA(())],
        input_output_aliases={len(args) - 1: 3} if aliased else {},
        compiler_params=_cparams(("arbitrary",)),
        name="moe_route",
    )(*args)


def _experts_kernel(te_ref, tr_ref, nv_ref, x_ref, wa_ref, wb_ref, wo_ref, y_ref, xs_scr, acc_scr, *, n_f):
    t = pl.program_id(0)
    j = pl.program_id(1)
    nv = nv_ref[t]

    @pl.when(nv > 0)
    def _():
        @pl.when(j == 0)
        def _():
            row = lax.broadcasted_iota(jnp.int32, xs_scr.shape, 0)
            xs_scr[...] = jnp.where(row < nv, x_ref[0].astype(F32), 0.0).astype(BF16)
            acc_scr[...] = jnp.zeros_like(acc_scr)

        xs = xs_scr[...]
        a = jnp.dot(xs, wa_ref[0], preferred_element_type=F32)
        b = jnp.dot(xs, wb_ref[0], preferred_element_type=F32)
        acc_scr[...] += jnp.dot((_silu(a) * b).astype(BF16), wo_ref[0], preferred_element_type=F32)

        @pl.when(j == n_f - 1)
        def _():
            y_ref[0] = acc_scr[...].astype(BF16)

    @pl.when((nv == 0) & (j == n_f - 1))
    def _():
        y_ref[0] = jnp.zeros(y_ref.shape[1:], BF16)


def _experts(rows, w_i, w_o, tile_e, tile_r, tile_nv):
    n_e, cap, d = rows.shape
    f = w_o.shape[1]
    tf = _hidden_tile(f, 1792)
    n_f = f // tf
    tm = MOE_TM
    n_t = tile_e.shape[0]

    def jv(t, j, nv):
        return jnp.where(nv[t] > 0, j, 0)

    return pl.pallas_call(
        functools.partial(_experts_kernel, n_f=n_f),
        grid_spec=pltpu.PrefetchScalarGridSpec(
            num_scalar_prefetch=3,
            grid=(n_t, n_f),
            in_specs=[pl.BlockSpec((1, tm, d), lambda t, j, te, tr, nv: (te[t], tr[t], 0)),
                      pl.BlockSpec((1, d, tf), lambda t, j, te, tr, nv: (te[t], 0, jv(t, j, nv))),
                      pl.BlockSpec((1, d, tf), lambda t, j, te, tr, nv: (te[t], 0, jv(t, j, nv) + n_f)),
                      pl.BlockSpec((1, tf, d), lambda t, j, te, tr, nv: (te[t], jv(t, j, nv), 0))],
            out_specs=pl.BlockSpec((1, tm, d), lambda t, j, te, tr, nv: (te[t], tr[t], 0)),
            scratch_shapes=[pltpu.VMEM((tm, d), BF16), pltpu.VMEM((tm, d), F32)]),
        out_shape=jax.ShapeDtypeStruct((n_e, cap, d), BF16),
        compiler_params=_cparams(("arbitrary", "arbitrary")),
        name="moe_experts",
    )(tile_e, tile_r, tile_nv, rows, w_i, w_i, w_o)


def _combine_kernel(meta_ref, y_ref, route_ref, x_ref, gf_ref, fg_ref, o_ref, yl_scr, sem, *, n_e, final):
    b = pl.program_id(0)
    tb = x_ref.shape[0]
    lr = yl_scr.shape[0]
    runs = []
    loc = 0
    for e in range(n_e):
        n = meta_ref[b, n_e + e]
        runs.append((n, meta_ref[b, e], loc))
        loc = loc + n
    for wait in (False, True):
        for e, (n, src0, dst0) in enumerate(runs):
            _run_copies(n, y_ref.at[e], src0, yl_scr, dst0, sem, wait)
    row = lax.broadcasted_iota(jnp.int32, yl_scr.shape, 0)
    yl = jnp.where(row < loc, yl_scr[...].astype(F32), 0.0).astype(BF16)
    route = route_ref[...]
    p1 = route[:, 0:1].astype(jnp.int32)
    p2 = route[:, 1:2].astype(jnp.int32)
    rr = lax.broadcasted_iota(jnp.int32, (tb, lr), 1)
    pw = jnp.where(rr == p1, route[:, 2:3], 0.0) + jnp.where(rr == p2, route[:, 3:4], 0.0)
    mix = jnp.dot(pw.astype(BF16), yl, preferred_element_type=F32)
    o_ref[...] = _finish(x_ref, gf_ref, fg_ref, mix, final)


def _combine(meta, y, route, x, gate_f, fg, n_e, tps, final):
    m, d = x.shape
    tb = MOE_TB
    lr = _moe_local_rows(n_e)
    return pl.pallas_call(
        functools.partial(_combine_kernel, n_e=n_e, final=final),
        grid_spec=pltpu.PrefetchScalarGridSpec(
            num_scalar_prefetch=1,
            grid=(m // tb,),
            in_specs=[pl.BlockSpec(memory_space=pl.ANY),
                      pl.BlockSpec((tb, LANES), lambda i, mt: (i, 0)),
                      pl.BlockSpec((tb, d), lambda i, mt: (i, 0)),
                      _mod_spec(gate_f, tb, tps),
                      pl.BlockSpec((1, d), lambda i, mt: (0, 0))],
            out_specs=pl.BlockSpec((tb, d), lambda i, mt: (i, 0)),
            scratch_shapes=[pltpu.VMEM((lr, d), BF16), pltpu.SemaphoreType.DMA(())]),
        out_shape=jax.ShapeDtypeStruct((m, d), F32),
        compiler_params=_cparams(("arbitrary",)),
        name="moe_combine",
    )(meta, y, route, x, gate_f, fg)


def _moe_sparse(streams, g, fg, w_r, w_i, w_o, final):
    n_e = w_o.shape[0]
    d = streams[0]["x"].shape[1]
    m_all = sum(s["x"].shape[0] for s in streams)
    n_b_all = m_all // MOE_TB
    worst_e = m_all + n_b_all * (MOE_GRAN - 1)
    worst = TOP_K * m_all + n_b_all * n_e * (MOE_GRAN - 1)
    cap = (-(-worst_e // MOE_TM) + 1) * MOE_TM
    n_t = -(-worst // MOE_TM) + n_e
    start = jnp.zeros((n_e,), jnp.int32)
    rows = None
    routed = []
    for s in streams:
        route, meta, start, rows = _route(s["x"], g, s["scale"], s["shift"], w_r, start, rows, n_e, cap, s["tps"])
        routed.append((route, meta))
    fill = start
    tiles_e = (fill + MOE_TM - 1) // MOE_TM
    first = jnp.cumsum(tiles_e) - tiles_e
    t = jnp.arange(n_t, dtype=jnp.int32)
    e_of = jnp.sum((t[:, None] >= (first + tiles_e)[None, :]).astype(jnp.int32), axis=1)
    valid = e_of < n_e
    e_cl = jnp.minimum(e_of, n_e - 1)
    r_of = t - first[e_cl]
    nv = jnp.where(valid, jnp.clip(fill[e_cl] - r_of * MOE_TM, 0, MOE_TM), 0).astype(jnp.int32)
    tile_e = jnp.where(valid, e_cl, n_e - 1).astype(jnp.int32)
    tile_r = jnp.where(valid, r_of, cap // MOE_TM - 1).astype(jnp.int32)
    y = _experts(rows, w_i, w_o, tile_e, tile_r, nv)
    return [_combine(meta, y, route, s["x"], s["gate_f"], fg, n_e, s["tps"], final)
            for s, (route, meta) in zip(streams, routed)]


def _prep_weights(w_in, w_branch, w_out, ffn_w_in, ffn_w_out, router_w, moe_w_in, moe_w_out):
    depth = w_in.shape[0]
    w_main = jnp.concatenate([w_in[..., OFF_GATE:], w_in[..., OFF_DN_QKV:OFF_DN_Z], w_in[..., OFF_AQ:OFF_GLU],
                              w_in[..., OFF_GLU:OFF_DN_QKV], w_in[..., OFF_DN_Z:OFF_DN_A]], axis=-1).astype(BF16)
    w_ab = jnp.pad(w_in[..., OFF_DN_A:OFF_GATE], ((0, 0), (0, 0), (0, LANES - 2 * DN_HEADS))).astype(BF16)
    w_r = jnp.pad(router_w, ((0, 0), (0, 0), (0, LANES - router_w.shape[-1])))
    return dict(w_main=w_main, w_ab=w_ab, w_branch=w_branch.astype(BF16), w_out=w_out.astype(BF16),
                ffn_w_in=ffn_w_in.astype(BF16), ffn_w_out=ffn_w_out.astype(BF16), w_r=w_r,
                moe_w_in=moe_w_in.astype(BF16), moe_w_out=moe_w_out.astype(BF16), depth=depth)


class _Stream:
    def __init__(self, x3, mod, caches):
        self.n_seq, self.seq_len, self.d = x3.shape
        self.m = self.n_seq * self.seq_len
        self.x = x3.reshape(self.m, self.d)
        self.mod = mod
        self.caches = caches
        self.decode = caches is not None
        self.states = ([], [], [], [], [])

    def tile(self, want):
        return self.m if self.decode else want

    def tps(self, t):
        return 1 if self.decode else self.seq_len // t

    def modv(self, l, idx, t):
        v = self.mod[l][:, idx]
        if self.decode:
            return jnp.repeat(v, self.seq_len, axis=0).reshape(self.m // t, t, self.d)
        return v.reshape(self.n_seq, 1, self.d)


def _mixers(s, l, pw, p):
    n_seq, seq_len, m, decode, caches = s.n_seq, s.seq_len, s.m, s.decode, s.caches
    x = s.x
    tm, tm_big = s.tile(512), s.tile(1024)
    zmain, zab = _in_proj(x, p["norm_g"][l, 0:1], s.modv(l, 1, tm_big), s.modv(l, 0, tm_big),
                          pw["w_main"][l], pw["w_ab"][l], tm_big, s.tps(tm_big))
    if decode:
        cache_k, cache_v = caches[0][l], caches[1][l]
        nr = cache_k.shape[1]
        tab = _bias_table(p["rel_bias"][l], PAST_LEN, seq_len, PAST_LEN - nr, nr + seq_len)
        att = _attn_decode(zmain, cache_k.reshape(n_seq, nr, ATT_W), cache_v.reshape(n_seq, nr, ATT_W),
                           tab[:, :, :nr], tab[:, :, nr:], n_seq, seq_len)
        keep = seq_len
    else:
        qb, tq = 2 * CHUNK, N_BACK * CHUNK
        tab = _bias_table(p["rel_bias"][l], tq, qb, 0, tq + qb)
        att = _attn_prompt(zmain, tab, n_seq, seq_len)
        keep = min(N_BACK * CHUNK, seq_len)
    z3 = zmain.reshape(n_seq, seq_len, Z_W)
    k_rows = z3[:, seq_len - keep:, Z_K:Z_K + ATT_W].astype(F32).reshape(n_seq, keep, ATT_HEADS, ATT_HEAD_DIM)
    v_rows = z3[:, seq_len - keep:, Z_V:Z_V + ATT_W].astype(F32).reshape(n_seq, keep, ATT_HEADS, ATT_HEAD_DIM)
    if decode:
        cbuf = jnp.pad(caches[2][l], ((0, 0), (HALO - (CONV_WIDTH - 1), 0), (0, 0)))
        t_rows, sub = seq_len, seq_len
    else:
        cbuf = jnp.zeros((n_seq, HALO, CONV_CH), F32)
        t_rows, sub = 512, 64
    cw = jnp.pad(p["conv_w"][l], ((0, HALO - CONV_WIDTH), (0, 0)))
    cv, new_cbuf = _conv_module(zmain, cbuf, cw, p["conv_b"][l][None], p["conv_ln_g"][l][None],
                                p["conv_ln_b"][l][None], n_seq, seq_len, t_rows, sub)
    if decode:
        dbuf = jnp.pad(caches[3][l], ((0, 0), (DN_HALO - (SHORT_CONV - 1), 0), (0, 0)))
        s0 = caches[4][l]
    else:
        dbuf = jnp.zeros((n_seq, DN_HALO, DN_CONV_CH), F32)
        s0 = jnp.zeros((n_seq, DN_HEADS, DN_DK, DN_DV), F32)
    cs = seq_len if seq_len <= CHUNK else CHUNK
    lane_pad = (0, LANES - DN_HEADS)
    expa = jnp.pad(jnp.exp(p["dn_A_log"][l]), lane_pad)[None]
    dtb = jnp.pad(p["dn_dt_bias"][l], lane_pad)[None]
    dn, new_dbuf, s_new = _deltanet(z3, zab.reshape(n_seq, seq_len, LANES), dbuf, s0, p["dn_conv_w"][l],
                                    expa, dtb, p["dn_norm_g"][l][None], cs, 4)
    s.x = _merge(att, cv, dn.reshape(m, DN_W), zmain, x, s.modv(l, 2, tm), pw["w_branch"][l], pw["w_out"][l],
                 tm, s.tps(tm))
    for lst, st in zip(s.states, (k_rows, v_rows, new_cbuf, new_dbuf, s_new)):
        lst.append(st)


def _channel_mixer(streams, l, pw, p, final):
    g = p["norm_g"][l, 1:2]
    fg = p["final_norm_g"][None]
    if l % 2 == 0:
        for s in streams:
            t = s.tile(512)
            s.x = _ffn(s.x, g, s.modv(l, 4, t), s.modv(l, 3, t), s.modv(l, 5, t), fg, pw["ffn_w_in"][l // 2],
                       pw["ffn_w_out"][l // 2], t, s.tps(t), final)
    else:
        t = MOE_TB
        parts = [dict(x=s.x, scale=s.modv(l, 4, t), shift=s.modv(l, 3, t), gate_f=s.modv(l, 5, t),
                      tps=s.tps(t)) for s in streams]
        for s, xn in zip(streams, _moe_sparse(parts, g, fg, pw["w_r"][l // 2], pw["moe_w_in"][l // 2],
                                              pw["moe_w_out"][l // 2], final)):
            s.x = xn


def kernel(x_prompt, x_sample, c_prompt, c_sample, cache_attn_k, cache_attn_v, state_conv, state_dn_conv, state_dn,
           w_ada, b_ada, norm_g, w_in, rel_bias, conv_w, conv_b, conv_ln_g, conv_ln_b, dn_conv_w, dn_A_log,
           dn_dt_bias, dn_norm_g, w_branch, w_out, ffn_w_in, ffn_w_out, router_w, moe_w_in, moe_w_out,
           final_norm_g):
    depth = w_in.shape[0]
    d = x_prompt.shape[-1]
    n_p = c_prompt.shape[0]
    p = dict(norm_g=norm_g, rel_bias=rel_bias, conv_w=conv_w, conv_b=conv_b, conv_ln_g=conv_ln_g,
             conv_ln_b=conv_ln_b, dn_conv_w=dn_conv_w, dn_A_log=dn_A_log, dn_dt_bias=dn_dt_bias,
             dn_norm_g=dn_norm_g, final_norm_g=final_norm_g)
    pw = _prep_weights(w_in, w_branch, w_out, ffn_w_in, ffn_w_out, router_w, moe_w_in, moe_w_out)
    mod = _ada(jnp.concatenate([c_prompt, c_sample], axis=0), w_ada, b_ada)
    mod = mod.reshape(depth, -1, 6, d)
    streams = [_Stream(x_prompt, mod[:, :n_p], None),
               _Stream(x_sample, mod[:, n_p:], (cache_attn_k, cache_attn_v, state_conv, state_dn_conv, state_dn))]
    for l in range(depth):
        for s in streams:
            _mixers(s, l, pw, p)
        _channel_mixer(streams, l, pw, p, final=l == depth - 1)
    ys = tuple(s.x.reshape(s.n_seq, s.seq_len, d) for s in streams)
    return ys + tuple(jnp.stack(st) for s in streams for st in s.states)
```

```python
import functools

import numpy as np
import jax
import jax.numpy as jnp
from jax import lax
from jax.experimental import pallas as pl
from jax.experimental.pallas import tpu as pltpu

F32 = jnp.float32
BF16 = jnp.bfloat16

PAST_LEN = 4096
CHUNK = 64
N_BACK = 8
ATT_HEADS = 8
ATT_HEAD_DIM = 64
ATT_W = ATT_HEADS * ATT_HEAD_DIM
REL_CLIP = 128
CONV_CH = 512
CONV_WIDTH = 31
DN_HEADS = 4
DN_DK = 128
DN_DV = 128
DN_QK_W = DN_HEADS * DN_DK
DN_W = DN_HEADS * DN_DV
DN_CONV_CH = 2 * DN_QK_W + DN_W
SHORT_CONV = 4
N_BRANCH = 3
BRANCH_W = 512
TOP_K = 2
NORM_EPS = 1e-6
NEG_BIG = -1e30

OFF_AQ = 0
OFF_GLU = 3 * ATT_W
OFF_DN_QKV = OFF_GLU + 2 * CONV_CH
OFF_DN_Z = OFF_DN_QKV + DN_CONV_CH
OFF_DN_A = OFF_DN_Z + DN_W
OFF_GATE = OFF_DN_A + 2 * DN_HEADS

Z_GATE = 0
Z_DNQKV = 3072
Z_Q = 4608
Z_K = 5120
Z_V = 5632
Z_U1 = 6144
Z_U2 = 6656
Z_DZ = 7168
Z_W = 7680
LANES = 128
SUBLANES = 8
HALO = 32
DN_HALO = 8

VMEM_LIMIT = 56 * 1024 * 1024


def _cparams(sem):
    return pltpu.CompilerParams(dimension_semantics=sem, vmem_limit_bytes=VMEM_LIMIT)


def _sigmoid(x):
    return 1.0 / (1.0 + jnp.exp(-x))


def _silu(x):
    return x * _sigmoid(x)


def _mod_norm(x, g, scale, shift):
    ms = jnp.mean(x * x, axis=-1, keepdims=True)
    return (x * lax.rsqrt(ms + NORM_EPS) * g) * (1.0 + scale) + shift


def _ada_kernel(c_ref, w_ref, b_ref, o_ref):
    s = _silu(c_ref[...])
    o_ref[0] = jnp.dot(s.astype(BF16), w_ref[0].astype(BF16), preferred_element_type=F32) + b_ref[0]


def _ada(c_all, w_ada, b_ada):
    depth, d, n6 = w_ada.shape
    n = c_all.shape[0]
    tn = 1024
    return pl.pallas_call(
        _ada_kernel,
        grid=(depth, n6 // tn),
        in_specs=[pl.BlockSpec((n, d), lambda l, j: (0, 0)),
                  pl.BlockSpec((1, d, tn), lambda l, j: (l, 0, j)),
                  pl.BlockSpec((1, 1, tn), lambda l, j: (l, 0, j))],
        out_specs=pl.BlockSpec((1, n, tn), lambda l, j: (l, 0, j)),
        out_shape=jax.ShapeDtypeStruct((depth, n, n6), F32),
        compiler_params=_cparams(("parallel", "parallel")),
        name="ada_mod",
    )(c_all, w_ada, b_ada.reshape(depth, 1, n6))


def _in_proj_kernel(x_ref, g_ref, sc_ref, sh_ref, w_ref, wab_ref, z_ref, zab_ref, h_scr):
    @pl.when(pl.program_id(1) == 0)
    def _():
        h = _mod_norm(x_ref[...], g_ref[...], sc_ref[0], sh_ref[0]).astype(BF16)
        h_scr[...] = h
        zab_ref[...] = jnp.dot(h, wab_ref[...], preferred_element_type=F32)

    z_ref[...] = jnp.dot(h_scr[...], w_ref[...], preferred_element_type=F32).astype(BF16)


def _mod_spec(mod, tm, tps):
    r = mod.shape[1]
    if r == 1:
        return pl.BlockSpec((1, 1, mod.shape[2]), lambda i, *_: (i // tps, 0, 0))
    return pl.BlockSpec((1, r, mod.shape[2]), lambda i, *_: (i, 0, 0))


def _in_proj(x, g, scale, shift, w_main, w_ab, tm, tps):
    m, d = x.shape
    tn = 1536
    nt = Z_W // tn
    return pl.pallas_call(
        _in_proj_kernel,
        grid=(m // tm, nt),
        in_specs=[pl.BlockSpec((tm, d), lambda i, j: (i, 0)),
                  pl.BlockSpec((1, d), lambda i, j: (0, 0)),
                  _mod_spec(scale, tm, tps), _mod_spec(shift, tm, tps),
                  pl.BlockSpec((d, tn), lambda i, j: (0, j)),
                  pl.BlockSpec((d, LANES), lambda i, j: (0, 0))],
        out_specs=[pl.BlockSpec((tm, tn), lambda i, j: (i, j)),
                   pl.BlockSpec((tm, LANES), lambda i, j: (i, 0))],
        out_shape=[jax.ShapeDtypeStruct((m, Z_W), BF16), jax.ShapeDtypeStruct((m, LANES), F32)],
        scratch_shapes=[pltpu.VMEM((tm, d), BF16)],
        compiler_params=_cparams(("parallel", "arbitrary")),
        name="in_proj",
    )(x, g, scale, shift, w_main, w_ab)


def _softmax_pv(parts, vs):
    m = parts[0].max(axis=-1, keepdims=True)
    for s in parts[1:]:
        m = jnp.maximum(m, s.max(axis=-1, keepdims=True))
    num = None
    den = None
    for s, v in zip(parts, vs):
        p = jnp.exp(s - m)
        l = p.sum(axis=-1, keepdims=True)
        o = jnp.dot(p.astype(BF16), v, preferred_element_type=F32)
        num = o if num is None else num + o
        den = l if den is None else den + l
    return num / den


def _qk(q, k):
    return lax.dot_general(q, k, (((1,), (1,)), ((), ())), preferred_element_type=F32)


def _attn_prompt_kernel(q_ref, kp_ref, kc_ref, vp_ref, vc_ref, tab_ref, o_ref, *, tps, tq, qb, win):
    first = (pl.program_id(0) % tps == 0).astype(F32)
    kwin = jnp.concatenate([kp_ref[...], kc_ref[...]], axis=0)
    vwin = jnp.concatenate([vp_ref[...], vc_ref[...]], axis=0)
    rowid = lax.broadcasted_iota(jnp.int32, (1, 2 * tq), 1)
    neg = jnp.where(rowid < tq, first * NEG_BIG, 0.0)
    lo = lax.broadcasted_iota(jnp.int32, (qb, LANES), 1) < ATT_HEAD_DIM
    scale = ATT_HEAD_DIM ** -0.5
    pairs = range(ATT_HEADS // 2)
    for b in range(tq // qb):
        negw = neg[:, b * qb:b * qb + win]
        rows = slice(b * qb, (b + 1) * qb)
        wrows = slice(b * qb, b * qb + win)
        s = []
        for hp in pairs:
            q = q_ref[rows, hp * LANES:(hp + 1) * LANES] * scale
            zero = jnp.zeros_like(q)
            q2 = jnp.concatenate([jnp.where(lo, q, zero), jnp.where(lo, zero, q)], axis=0)
            s.append(_qk(q2, kwin[wrows, hp * LANES:(hp + 1) * LANES]) + tab_ref[hp] + negw)
        p = [jnp.exp(s[hp] - s[hp].max(axis=-1, keepdims=True)) for hp in pairs]
        den = [p[hp].sum(axis=-1, keepdims=True) for hp in pairs]
        num = [jnp.dot(p[hp].astype(BF16), vwin[wrows, hp * LANES:(hp + 1) * LANES], preferred_element_type=F32)
               for hp in pairs]
        for hp in pairs:
            o2 = num[hp] / den[hp]
            o_ref[rows, hp * LANES:(hp + 1) * LANES] = jnp.where(lo, o2[:qb], o2[qb:]).astype(BF16)


def _attn_prompt(zmain, tab, n_seq, seq_len):
    m = zmain.shape[0]
    tq = N_BACK * CHUNK
    qb = 2 * CHUNK
    win = tq + qb
    tps = seq_len // tq
    cq, ck, cv = Z_Q // ATT_W, Z_K // ATT_W, Z_V // ATT_W

    def prev(i):
        return jnp.where(i % tps == 0, i, i - 1)

    return pl.pallas_call(
        functools.partial(_attn_prompt_kernel, tps=tps, tq=tq, qb=qb, win=win),
        grid=(m // tq,),
        in_specs=[pl.BlockSpec((tq, ATT_W), lambda i: (i, cq)),
                  pl.BlockSpec((tq, ATT_W), lambda i: (prev(i), ck)),
                  pl.BlockSpec((tq, ATT_W), lambda i: (i, ck)),
                  pl.BlockSpec((tq, ATT_W), lambda i: (prev(i), cv)),
                  pl.BlockSpec((tq, ATT_W), lambda i: (i, cv)),
                  pl.BlockSpec((ATT_HEADS // 2, 2 * qb, win), lambda i: (0, 0, 0))],
        out_specs=pl.BlockSpec((tq, ATT_W), lambda i: (i, 0)),
        out_shape=jax.ShapeDtypeStruct((m, ATT_W), BF16),
        compiler_params=_cparams(("parallel",)),
        name="attn_prompt",
    )(zmain, zmain, zmain, zmain, zmain, tab.reshape(ATT_HEADS // 2, 2 * qb, win))


def _attn_decode_kernel(q_ref, kc_ref, kn_ref, vc_ref, vn_ref, tabc_ref, tabn_ref, o_ref):
    lq = q_ref.shape[0]
    kc = kc_ref[0].astype(BF16)
    vc = vc_ref[0].astype(BF16)
    lo = lax.broadcasted_iota(jnp.int32, (lq, LANES), 1) < ATT_HEAD_DIM
    scale = ATT_HEAD_DIM ** -0.5
    for hp in range(ATT_HEADS // 2):
        cols = slice(hp * LANES, (hp + 1) * LANES)
        q = q_ref[:, cols]
        outs = []
        for half in range(2):
            qm = jnp.where(lo if half == 0 else jnp.logical_not(lo), q, jnp.zeros_like(q))
            s1 = _qk(qm, kc[:, cols]) * scale + tabc_ref[2 * hp + half]
            s2 = _qk(qm, kn_ref[:, cols]) * scale + tabn_ref[2 * hp + half]
            outs.append(_softmax_pv([s1, s2], [vc[:, cols], vn_ref[:, cols]]))
        o_ref[:, cols] = jnp.where(lo, outs[0], outs[1]).astype(BF16)


def _attn_decode(zmain, cache_k, cache_v, tab_c, tab_n, n_seq, lq):
    m = zmain.shape[0]
    nr = cache_k.shape[1]
    cq, ck, cv = Z_Q // ATT_W, Z_K // ATT_W, Z_V // ATT_W
    return pl.pallas_call(
        _attn_decode_kernel,
        grid=(n_seq,),
        in_specs=[pl.BlockSpec((lq, ATT_W), lambda i: (i, cq)),
                  pl.BlockSpec((1, nr, ATT_W), lambda i: (i, 0, 0)),
                  pl.BlockSpec((lq, ATT_W), lambda i: (i, ck)),
                  pl.BlockSpec((1, nr, ATT_W), lambda i: (i, 0, 0)),
                  pl.BlockSpec((lq, ATT_W), lambda i: (i, cv)),
                  pl.BlockSpec((ATT_HEADS, lq, nr), lambda i: (0, 0, 0)),
                  pl.BlockSpec((ATT_HEADS, lq, lq), lambda i: (0, 0, 0))],
        out_specs=pl.BlockSpec((lq, ATT_W), lambda i: (i, 0)),
        out_shape=jax.ShapeDtypeStruct((m, ATT_W), BF16),
        compiler_params=_cparams(("parallel",)),
        name="attn_decode",
    )(zmain, cache_k, zmain, cache_v, zmain, tab_c, tab_n)


def _bias_table(rel_bias_l, qpos0, nq, kpos0, nk):
    qpos = qpos0 + np.arange(nq)
    kpos = kpos0 + np.arange(nk)
    qc = (qpos // CHUNK)[:, None]
    kc = (kpos // CHUNK)[None, :]
    ok = (kpos[None, :] >= 0) & (kc <= qc) & (kc >= qc - N_BACK)
    t_min = qpos0 - (kpos0 + nk - 1)
    t_max = qpos0 + nq - 1 - kpos0
    idx = np.clip(np.arange(t_max, t_min - 1, -1), -REL_CLIP, REL_CLIP) + REL_CLIP
    grev = rel_bias_l.astype(F32)[:, idx]
    n_h, l_all = grev.shape
    flat = jnp.tile(jnp.pad(grev, ((0, 0), (0, 1))), (1, nq))[:, :nq * l_all]
    tab = flat.reshape(n_h, nq, l_all)[:, :, nq - 1:nq - 1 + nk]
    return jnp.where(ok[None], tab, NEG_BIG)


def _conv_kernel(u1_ref, u2_ref, buf_ref, w_ref, cb_ref, lg_ref, lb_ref, o_ref, nb_ref, xp_scr, sh_scr,
                 *, t_rows, sub, n_t):
    t = pl.program_id(1)

    @pl.when(t == 0)
    def _():
        xp_scr[0:HALO, :] = buf_ref[0]

    u1 = u1_ref[...].astype(F32)
    u2 = u2_ref[...].astype(F32)
    xp_scr[HALO:HALO + t_rows, :] = u1 * _sigmoid(u2)
    off = HALO - (CONV_WIDTH - 1)
    n_sh = HALO + t_rows - SUBLANES
    for r in range(1, SUBLANES):
        sh_scr[r - 1, 0:n_sh, :] = xp_scr[r:r + n_sh, :]
    for r0 in range(0, t_rows, sub):
        acc = None
        for j in range(CONV_WIDTH):
            a, r = divmod(off + j, SUBLANES)
            src = xp_scr[r0 + a * SUBLANES:r0 + a * SUBLANES + sub, :] if r == 0 else \
                sh_scr[r - 1, r0 + a * SUBLANES:r0 + a * SUBLANES + sub, :]
            term = w_ref[j:j + 1, :] * src
            acc = term if acc is None else acc + term
        cv = acc + cb_ref[...]
        mu = jnp.mean(cv, axis=-1, keepdims=True)
        cen = cv - mu
        var = jnp.mean(cen * cen, axis=-1, keepdims=True)
        y = cen * lax.rsqrt(var + NORM_EPS) * lg_ref[...] + lb_ref[...]
        o_ref[r0:r0 + sub, :] = _silu(y).astype(BF16)

    @pl.when(t == n_t - 1)
    def _():
        nb_ref[0] = xp_scr[t_rows + off:t_rows + HALO, :]

    xp_scr[0:HALO, :] = xp_scr[t_rows:t_rows + HALO, :]


def _conv_module(zmain, buf, w, cb, lg, lb, n_seq, seq_len, t_rows, sub):
    m = zmain.shape[0]
    n_t = seq_len // t_rows
    c1, c2 = Z_U1 // CONV_CH, Z_U2 // CONV_CH
    vec = pl.BlockSpec((1, CONV_CH), lambda n, t: (0, 0))
    return pl.pallas_call(
        functools.partial(_conv_kernel, t_rows=t_rows, sub=sub, n_t=n_t),
        grid=(n_seq, n_t),
        in_specs=[pl.BlockSpec((t_rows, CONV_CH), lambda n, t: (n * n_t + t, c1)),
                  pl.BlockSpec((t_rows, CONV_CH), lambda n, t: (n * n_t + t, c2)),
                  pl.BlockSpec((1, HALO, CONV_CH), lambda n, t: (n, 0, 0)),
                  pl.BlockSpec((HALO, CONV_CH), lambda n, t: (0, 0)),
                  vec, vec, vec],
        out_specs=[pl.BlockSpec((t_rows, CONV_CH), lambda n, t: (n * n_t + t, 0)),
                   pl.BlockSpec((1, CONV_WIDTH - 1, CONV_CH), lambda n, t: (n, 0, 0))],
        out_shape=[jax.ShapeDtypeStruct((m, CONV_CH), BF16),
                   jax.ShapeDtypeStruct((n_seq, CONV_WIDTH - 1, CONV_CH), F32)],
        scratch_shapes=[pltpu.VMEM((HALO + t_rows, CONV_CH), F32),
                        pltpu.VMEM((SUBLANES - 1, HALO + t_rows - SUBLANES, CONV_CH), F32)],
        compiler_params=_cparams(("parallel", "arbitrary")),
        name="conv_module",
    )(zmain, zmain, buf, w, cb, lg, lb)


def _split3(x):
    hi = x.astype(BF16)
    r = x - hi.astype(F32)
    mid = r.astype(BF16)
    lo = (r - mid.astype(F32)).astype(BF16)
    return hi, mid, lo


def _dn_kernel(qkv_ref, dz_ref, ab_ref, buf_ref, s0_ref, cw_ref, expa_ref, dtb_ref, ng_ref,
               o_ref, nbuf_ref, sfin_ref, xp_scr, s_scr, *, nb, cs, n_c):
    c = pl.program_id(1)

    @pl.when(c == 0)
    def _():
        xp_scr[:, 0:DN_HALO, :] = buf_ref[...]
        xp_scr[:, DN_HALO:2 * DN_HALO, :] = jnp.zeros((nb, DN_HALO, DN_CONV_CH), F32)
        s_scr[...] = s0_ref[...]

    lg = cs.bit_length() - 1
    si = lax.broadcasted_iota(jnp.int32, (cs, cs), 0)
    sj = lax.broadcasted_iota(jnp.int32, (cs, cs), 1)
    shift = jnp.concatenate([jnp.where(sj == si - k, 1.0, 0.0) for k in range(SHORT_CONV - 1, 0, -1)],
                            axis=0).astype(BF16)
    wd = DN_HEADS * cs
    ii = lax.broadcasted_iota(jnp.int32, (cs, cs), 0)
    jj = lax.broadcasted_iota(jnp.int32, (cs, cs), 1)
    lincl = jnp.where(ii >= jj, 1.0, 0.0).astype(BF16)
    r = lax.broadcasted_iota(jnp.int32, (cs, wd), 0)
    col = lax.broadcasted_iota(jnp.int32, (cs, wd), 1)
    jl = col & (cs - 1)
    hid = col >> lg
    incl_s = r >= jl
    strict_s = r > jl
    diag_s = r == jl
    rb = lax.broadcasted_iota(jnp.int32, (wd, wd), 0)
    cb = lax.broadcasted_iota(jnp.int32, (wd, wd), 1)
    same_head = (rb >> lg) == (cb >> lg)
    bd_mask = jnp.where(same_head, 1.0, 0.0).astype(BF16)
    kb_mask = jnp.where((lax.broadcasted_iota(jnp.int32, (wd, DN_QK_W), 0) >> lg) == (
        lax.broadcasted_iota(jnp.int32, (wd, DN_QK_W), 1) // DN_DK), 1.0, 0.0).astype(BF16)

    def level_sel(row, colm, sh):
        bi = row >> sh
        return jnp.where((bi & 1) == 1, bi - 1, -1) == (colm >> sh)

    level_mask = [jnp.where(same_head & level_sel(rb & (cs - 1), cb & (cs - 1), sh), 1.0, 0.0).astype(BF16)
                  for sh in range(1, lg)]

    def bd(m):
        return jnp.concatenate([m.astype(BF16)] * DN_HEADS, axis=0) * bd_mask

    off = DN_HALO - (SHORT_CONV - 1)
    heads = range(DN_HEADS)
    a_bd, d, qk, rhs, qg, kd, gl, dz = [], [], [], [], [], [], [], []
    for b in range(nb):
        xb = qkv_ref[b]
        xf = xb.astype(F32)
        sh3 = jnp.dot(shift, xb, preferred_element_type=F32)
        y = cw_ref[SHORT_CONV - 1:SHORT_CONV, :] * xf
        for j in range(SHORT_CONV - 1):
            y = y + cw_ref[j:j + 1, :] * sh3[j * cs:(j + 1) * cs]
        head = y[:DN_HALO]
        for j in range(SHORT_CONV - 1):
            head = head + cw_ref[j:j + 1, :] * xp_scr[b, off + j:off + j + DN_HALO, :]
        y = _silu(jnp.concatenate([head, y[DN_HALO:]], axis=0))
        xp_scr[b, 0:DN_HALO, :] = xf[cs - DN_HALO:]
        ab = ab_ref[b]
        xg = ab + dtb_ref[...]
        softplus = jnp.maximum(xg, 0.0) + jnp.log(1.0 + jnp.exp(-jnp.abs(xg)))
        gfull = -expa_ref[...] * softplus
        beta_full = _sigmoid(ab)
        gc3 = jnp.dot(lincl, jnp.concatenate(_split3(gfull), axis=1), preferred_element_type=F32)
        gcum = gc3[:, :LANES] + gc3[:, LANES:2 * LANES] + gc3[:, 2 * LANES:]
        dz.append(dz_ref[b].astype(F32))
        q, k, kb, kbg, vb, qg_b, kd_b, gl_b = [], [], [], [], [], [], [], []
        gcx = None
        for h in heads:
            qh = y[:, h * DN_DK:(h + 1) * DN_DK]
            kh = y[:, DN_QK_W + h * DN_DK:DN_QK_W + (h + 1) * DN_DK]
            vh = y[:, 2 * DN_QK_W + h * DN_DV:2 * DN_QK_W + (h + 1) * DN_DV]
            qh = qh * (lax.rsqrt(jnp.sum(qh * qh, axis=-1, keepdims=True) + 1e-6) * (DN_DK ** -0.5))
            kh = kh * lax.rsqrt(jnp.sum(kh * kh, axis=-1, keepdims=True) + 1e-6)
            gc = gcum[:, h:h + 1]
            beta = beta_full[:, DN_HEADS + h:DN_HEADS + h + 1]
            glast = gcum[cs - 1:cs, h:h + 1]
            eg = jnp.exp(gc)
            gcx = gc if gcx is None else jnp.where(hid == h, gc, gcx)
            q.append(qh)
            k.append(kh)
            kb.append(kh * beta)
            kbg.append(kh * (beta * eg))
            vb.append(vh * beta)
            qg_b.append(qh * eg)
            kd_b.append(kh * jnp.exp(glast - gc))
            gl_b.append(jnp.exp(glast))
        gcx = jnp.broadcast_to(gcx, (cs, wd))
        grx = jnp.sum(jnp.where(diag_s, gcx, 0.0), axis=0, keepdims=True)
        decay = jnp.exp(jnp.where(incl_s, gcx - grx, NEG_BIG))
        x_all = jnp.concatenate([jnp.concatenate(kb, axis=1), jnp.concatenate(q, axis=1)], axis=0).astype(BF16)
        k_bd = jnp.concatenate([jnp.concatenate(k, axis=1).astype(BF16)] * DN_HEADS, axis=0) * kb_mask
        kk = _qk(x_all, k_bd)
        a = jnp.where(strict_s, kk[:cs] * decay, 0.0)
        a_bd.append(bd(a))
        d.append(jnp.where(diag_s, 1.0, 0.0) - jnp.where(level_sel(r, jl, 0), a, 0.0))
        qk.append(kk[cs:] * decay)
        rhs.append(jnp.concatenate([jnp.concatenate([kbg[h], vb[h]], axis=1) for h in heads],
                                   axis=0).astype(BF16))
        qg.append(qg_b)
        kd.append(kd_b)
        gl.append(gl_b)
    for lm in level_mask:
        for b in range(nb):
            x = jnp.dot(d[b].astype(BF16), a_bd[b] * lm, preferred_element_type=F32)
            d[b] = d[b] - jnp.dot(x.astype(BF16), bd(d[b]), preferred_element_type=F32)
    wu = [jnp.dot(bd(d[b]), rhs[b], preferred_element_type=F32) for b in range(nb)]
    s_old = [[s_scr[b, h] for h in heads] for b in range(nb)]
    wq = [[jnp.dot(jnp.concatenate([wu[b][h * cs:(h + 1) * cs, :DN_DK], qg[b][h]], axis=0).astype(BF16),
                   s_old[b][h].astype(BF16), preferred_element_type=F32) for h in heads] for b in range(nb)]
    v_new = [[wu[b][h * cs:(h + 1) * cs, DN_DK:] - wq[b][h][:cs] for h in heads] for b in range(nb)]
    o2 = [jnp.dot(bd(qk[b]), jnp.concatenate(v_new[b], axis=0).astype(BF16), preferred_element_type=F32)
          for b in range(nb)]
    for b in range(nb):
        for h in heads:
            o = wq[b][h][cs:] + o2[b][h * cs:(h + 1) * cs]
            s_scr[b, h] = s_old[b][h] * gl[b][h] + lax.dot_general(
                kd[b][h].astype(BF16), v_new[b][h].astype(BF16), (((0,), (0,)), ((), ())),
                preferred_element_type=F32)
            on = o * lax.rsqrt(jnp.mean(o * o, axis=-1, keepdims=True) + NORM_EPS) * ng_ref[...]
            dzh = dz[b][:, h * DN_DV:(h + 1) * DN_DV]
            o_ref[b, :, h * DN_DV:(h + 1) * DN_DV] = (on * _silu(dzh)).astype(BF16)

    @pl.when(c == n_c - 1)
    def _():
        nbuf_ref[...] = xp_scr[:, off:DN_HALO, :]
        sfin_ref[...] = s_scr[...]


def _deltanet(zmain3, zab3, buf, s0, cw, expa, dtb, ng, cs, nb):
    n_seq, seq_len, _ = zmain3.shape
    n_c = seq_len // cs
    cq, cz = Z_DNQKV // DN_CONV_CH, Z_DZ // DN_W
    row = pl.BlockSpec((1, LANES), lambda s, c: (0, 0))
    return pl.pallas_call(
        functools.partial(_dn_kernel, nb=nb, cs=cs, n_c=n_c),
        grid=(n_seq // nb, n_c),
        in_specs=[pl.BlockSpec((nb, cs, DN_CONV_CH), lambda s, c: (s, c, cq)),
                  pl.BlockSpec((nb, cs, DN_W), lambda s, c: (s, c, cz)),
                  pl.BlockSpec((nb, cs, LANES), lambda s, c: (s, c, 0)),
                  pl.BlockSpec((nb, DN_HALO, DN_CONV_CH), lambda s, c: (s, 0, 0)),
                  pl.BlockSpec((nb, DN_HEADS, DN_DK, DN_DV), lambda s, c: (s, 0, 0, 0)),
                  pl.BlockSpec((SHORT_CONV, DN_CONV_CH), lambda s, c: (0, 0)),
                  row, row, row],
        out_specs=[pl.BlockSpec((nb, cs, DN_W), lambda s, c: (s, c, 0)),
                   pl.BlockSpec((nb, SHORT_CONV - 1, DN_CONV_CH), lambda s, c: (s, 0, 0)),
                   pl.BlockSpec((nb, DN_HEADS, DN_DK, DN_DV), lambda s, c: (s, 0, 0, 0))],
        out_shape=[jax.ShapeDtypeStruct((n_seq, seq_len, DN_W), BF16),
                   jax.ShapeDtypeStruct((n_seq, SHORT_CONV - 1, DN_CONV_CH), F32),
                   jax.ShapeDtypeStruct((n_seq, DN_HEADS, DN_DK, DN_DV), F32)],
        scratch_shapes=[pltpu.VMEM((nb, 2 * DN_HALO, DN_CONV_CH), F32),
                        pltpu.VMEM((nb, DN_HEADS, DN_DK, DN_DV), F32)],
        compiler_params=_cparams(("parallel", "arbitrary")),
        name="deltanet",
    )(zmain3, zmain3, zab3, buf, s0, cw, expa, dtb, ng)


def _merge_kernel(att_ref, cv_ref, dn_ref, g0_ref, g1_ref, g2_ref, x_ref, gm_ref, wb_ref, wo_ref, o_ref):
    acc = None
    for b, (br, gr) in enumerate(((att_ref, g0_ref), (cv_ref, g1_ref), (dn_ref, g2_ref))):
        yb = jnp.dot(br[...], wb_ref[b], preferred_element_type=F32)
        t = _sigmoid(gr[...].astype(F32)) * yb
        acc = t if acc is None else acc + t
    out = jnp.dot(acc.astype(BF16), wo_ref[...], preferred_element_type=F32)
    o_ref[...] = x_ref[...] + gm_ref[0] * out


def _merge(att, cv, dn, zmain, x, gate_m, wb, wo, tm, tps):
    m, d = x.shape
    br = pl.BlockSpec((tm, BRANCH_W), lambda i: (i, 0))
    return pl.pallas_call(
        _merge_kernel,
        grid=(m // tm,),
        in_specs=[br, br, br,
                  pl.BlockSpec((tm, d), lambda i: (i, 0)),
                  pl.BlockSpec((tm, d), lambda i: (i, 1)),
                  pl.BlockSpec((tm, d), lambda i: (i, 2)),
                  pl.BlockSpec((tm, d), lambda i: (i, 0)),
                  _mod_spec(gate_m, tm, tps),
                  pl.BlockSpec((N_BRANCH, BRANCH_W, d), lambda i: (0, 0, 0)),
                  pl.BlockSpec((d, d), lambda i: (0, 0))],
        out_specs=pl.BlockSpec((tm, d), lambda i: (i, 0)),
        out_shape=jax.ShapeDtypeStruct((m, d), F32),
        compiler_params=_cparams(("parallel",)),
        name="merge",
    )(att, cv, dn, zmain, zmain, zmain, x, gate_m, wb, wo)


def _finish(x_ref, gf_ref, fg_ref, acc, final):
    xn = x_ref[...] + gf_ref[0] * acc
    if final:
        ms = jnp.mean(xn * xn, axis=-1, keepdims=True)
        xn = xn * lax.rsqrt(ms + NORM_EPS) * fg_ref[...]
    return xn


def _ffn_kernel(x_ref, g_ref, sc_ref, sh_ref, gf_ref, fg_ref, wa_ref, wb_ref, wo_ref, o_ref, h_scr, acc_scr,
                *, n_f, final):
    j = pl.program_id(1)

    @pl.when(j == 0)
    def _():
        h_scr[...] = _mod_norm(x_ref[...], g_ref[...], sc_ref[0], sh_ref[0]).astype(BF16)
        acc_scr[...] = jnp.zeros_like(acc_scr)

    h = h_scr[...]
    a = jnp.dot(h, wa_ref[...], preferred_element_type=F32)
    b = jnp.dot(h, wb_ref[...], preferred_element_type=F32)
    acc_scr[...] += jnp.dot((_silu(a) * b).astype(BF16), wo_ref[...], preferred_element_type=F32)

    @pl.when(j == n_f - 1)
    def _():
        o_ref[...] = _finish(x_ref, gf_ref, fg_ref, acc_scr[...], final)


def _hidden_tile(f, limit):
    return max(t for t in range(LANES, min(f, limit) + 1, LANES) if f % t == 0)


def _ffn(x, g, scale, shift, gate_f, fg, w_i, w_o, tm, tps, final):
    m, d = x.shape
    f = w_o.shape[0]
    tf = _hidden_tile(f, 1408)
    n_f = f // tf
    return pl.pallas_call(
        functools.partial(_ffn_kernel, n_f=n_f, final=final),
        grid=(m // tm, n_f),
        in_specs=[pl.BlockSpec((tm, d), lambda i, j: (i, 0)),
                  pl.BlockSpec((1, d), lambda i, j: (0, 0)),
                  _mod_spec(scale, tm, tps), _mod_spec(shift, tm, tps), _mod_spec(gate_f, tm, tps),
                  pl.BlockSpec((1, d), lambda i, j: (0, 0)),
                  pl.BlockSpec((d, tf), lambda i, j: (0, j)),
                  pl.BlockSpec((d, tf), lambda i, j: (0, j + n_f)),
                  pl.BlockSpec((tf, d), lambda i, j: (j, 0))],
        out_specs=pl.BlockSpec((tm, d), lambda i, j: (i, 0)),
        out_shape=jax.ShapeDtypeStruct((m, d), F32),
        scratch_shapes=[pltpu.VMEM((tm, d), BF16), pltpu.VMEM((tm, d), F32)],
        compiler_params=_cparams(("parallel", "arbitrary")),
        name="ffn",
    )(x, g, scale, shift, gate_f, fg, w_i, w_i, w_o)


MOE_TB = 512
MOE_GRAN = 16
MOE_TM = 512
MOE_RUN_BITS = 6


def _moe_local_rows(n_e):
    return TOP_K * MOE_TB + n_e * MOE_GRAN


def _run_copies(n, src_ref, src0, dst_ref, dst0, sem, wait):
    off = 0
    for bit in reversed(range(MOE_RUN_BITS)):
        size = MOE_GRAN << bit
        take = (n & size) != 0
        cp = pltpu.make_async_copy(src_ref.at[pl.ds(pl.multiple_of(src0 + off, MOE_GRAN), size)],
                                   dst_ref.at[pl.ds(pl.multiple_of(dst0 + off, MOE_GRAN), size)], sem)

        @pl.when(take)
        def _():
            if wait:
                cp.wait()
            else:
                cp.start()

        off = off + jnp.where(take, size, 0)


def _route_kernel(start_ref, x_ref, g_ref, sc_ref, sh_ref, wr_ref, *rest, n_e, n_steps, nk, aliased):
    if aliased:
        rest = rest[1:]
    route_ref, meta_ref, fill_ref, rows_ref, sorted_scr, cur_scr, sem = rest
    step = pl.program_id(0)
    tb = x_ref.shape[0] // nk
    lr = sorted_scr.shape[1]
    blocks = range(nk)

    @pl.when(step == 0)
    def _():
        for e in range(n_e):
            cur_scr[e] = start_ref[e]

    def rows_of(ref0, k):
        return ref0 if ref0.shape[0] == 1 else ref0[k * tb:(k + 1) * tb]

    sc, sh = sc_ref[0], sh_ref[0]
    h = [_mod_norm(x_ref[k * tb:(k + 1) * tb, :], g_ref[...], rows_of(sc, k), rows_of(sh, k)) for k in blocks]
    logits = [jnp.dot(h[k], wr_ref[...], preferred_element_type=F32, precision=lax.Precision.HIGHEST)
              for k in blocks]
    lane = lax.broadcasted_iota(jnp.int32, (tb, LANES), 1)
    ti = lax.broadcasted_iota(jnp.int32, (tb, tb), 0)
    tj = lax.broadcasted_iota(jnp.int32, (tb, tb), 1)
    earlier = jnp.where(ti > tj, 1.0, 0.0).astype(BF16)
    ei = lax.broadcasted_iota(jnp.int32, (LANES, LANES), 0)
    ej = lax.broadcasted_iota(jnp.int32, (LANES, LANES), 1)
    before = jnp.where(ei < ej, 1.0, 0.0).astype(BF16)
    rr = lax.broadcasted_iota(jnp.int32, (tb, lr), 1)
    i1, i2, w1, w2, sel = [], [], [], [], []
    for k in blocks:
        lg = jnp.where(lane < n_e, logits[k], NEG_BIG)
        m1 = lg.max(axis=-1, keepdims=True)
        a1 = jnp.where(lg == m1, lane, LANES).min(axis=-1, keepdims=True)
        lg2 = jnp.where(lane == a1, NEG_BIG, lg)
        m2 = lg2.max(axis=-1, keepdims=True)
        a2 = jnp.where(lg2 == m2, lane, LANES).min(axis=-1, keepdims=True)
        e2 = jnp.exp(m2 - m1)
        i1.append(a1)
        i2.append(a2)
        w1.append(1.0 / (1.0 + e2))
        w2.append(e2 / (1.0 + e2))
        sel.append(jnp.where(lane == a1, 1.0, 0.0) + jnp.where(lane == a2, 1.0, 0.0))
    rank = [jnp.dot(earlier, sel[k].astype(BF16), preferred_element_type=F32) for k in blocks]
    cpad = [jnp.floor((jnp.sum(sel[k], axis=0, keepdims=True) + (MOE_GRAN - 1)) * (1.0 / MOE_GRAN)) * MOE_GRAN
            for k in blocks]
    loc = [jnp.dot(jnp.broadcast_to(cpad[k], (SUBLANES, LANES)).astype(BF16), before,
                   preferred_element_type=F32)[0:1] for k in blocks]
    for k in blocks:
        pos = loc[k] + rank[k]
        p1 = jnp.sum(jnp.where(lane == i1[k], pos, 0.0), axis=-1, keepdims=True)
        p2 = jnp.sum(jnp.where(lane == i2[k], pos, 0.0), axis=-1, keepdims=True)
        route_ref[k * tb:(k + 1) * tb, :] = jnp.where(
            lane == 0, p1, jnp.where(lane == 1, p2, jnp.where(lane == 2, w1[k], jnp.where(lane == 3, w2[k], 0.0))))
        perm = jnp.where((rr == p1.astype(jnp.int32)) | (rr == p2.astype(jnp.int32)), 1.0, 0.0).astype(BF16)
        sorted_scr[k] = lax.dot_general(perm, h[k].astype(BF16), (((0,), (0,)), ((), ())),
                                        preferred_element_type=F32).astype(BF16)
    runs = []
    for k in blocks:
        cp_i = cpad[k].astype(jnp.int32)
        loc_i = loc[k].astype(jnp.int32)
        blk = step * nk + k
        for e in range(n_e):
            n = cp_i[0, e]
            dst0 = cur_scr[e]
            meta_ref[blk, e] = dst0
            meta_ref[blk, n_e + e] = n
            cur_scr[e] = dst0 + n
            runs.append((k, e, n, loc_i[0, e], dst0))
    for wait in (False, True):
        for k, e, n, src0, dst0 in runs:
            _run_copies(n, sorted_scr.at[k], src0, rows_ref.at[e], dst0, sem, wait)

    @pl.when(step == n_steps - 1)
    def _():
        for e in range(n_e):
            fill_ref[e] = cur_scr[e]


def _route(x, g, scale, shift, w_r, start, rows_in, n_e, cap, tps):
    m, d = x.shape
    tb = MOE_TB
    n_b = m // tb
    nk = 2 if (n_b % 2 == 0 and tps % 2 == 0) else 1
    lr = _moe_local_rows(n_e)
    aliased = rows_in is not None
    smem = pl.BlockSpec(memory_space=pltpu.SMEM)
    hbm = pl.BlockSpec(memory_space=pl.ANY)
    in_specs = [smem,
                pl.BlockSpec((nk * tb, d), lambda i: (i, 0)),
                pl.BlockSpec((1, d), lambda i: (0, 0)),
                _mod_spec(scale, nk * tb, tps // nk), _mod_spec(shift, nk * tb, tps // nk),
                pl.BlockSpec((d, LANES), lambda i: (0, 0))]
    args = [start, x, g, scale, shift, w_r]
    if aliased:
        in_specs.append(hbm)
        args.append(rows_in)
    return pl.pallas_call(
        functools.partial(_route_kernel, n_e=n_e, n_steps=n_b // nk, nk=nk, aliased=aliased),
        grid=(n_b // nk,),
        in_specs=in_specs,
        out_specs=[pl.BlockSpec((nk * tb, LANES), lambda i: (i, 0)), smem, smem, hbm],
        out_shape=[jax.ShapeDtypeStruct((m, LANES), F32),
                   jax.ShapeDtypeStruct((n_b, 2 * n_e), jnp.int32),
                   jax.ShapeDtypeStruct((n_e,), jnp.int32),
                   jax.ShapeDtypeStruct((n_e, cap, d), BF16)],
        scratch_shapes=[pltpu.VMEM((nk, lr, d), BF16), pltpu.SMEM((n_e,), jnp.int32), pltpu.SemaphoreType.DMA(())],
        input_output_aliases={len(args) - 1: 3} if aliased else {},
        compiler_params=_cparams(("arbitrary",)),
        name="moe_route",
    )(*args)


def _experts_kernel(te_ref, tr_ref, nv_ref, x_ref, wa_ref, wb_ref, wo_ref, y_ref, xs_scr, acc_scr, *, n_f):
    t = pl.program_id(0)
    j = pl.program_id(1)
    nv = nv_ref[t]

    @pl.when(nv > 0)
    def _():
        @pl.when(j == 0)
        def _():
            row = lax.broadcasted_iota(jnp.int32, xs_scr.shape, 0)
            xs_scr[...] = jnp.where(row < nv, x_ref[0].astype(F32), 0.0).astype(BF16)
            acc_scr[...] = jnp.zeros_like(acc_scr)

        xs = xs_scr[...]
        a = jnp.dot(xs, wa_ref[0], preferred_element_type=F32)
        b = jnp.dot(xs, wb_ref[0], preferred_element_type=F32)
        acc_scr[...] += jnp.dot((_silu(a) * b).astype(BF16), wo_ref[0], preferred_element_type=F32)

        @pl.when(j == n_f - 1)
        def _():
            y_ref[0] = acc_scr[...].astype(BF16)

    @pl.when((nv == 0) & (j == n_f - 1))
    def _():
        y_ref[0] = jnp.zeros(y_ref.shape[1:], BF16)


def _experts(rows, w_i, w_o, tile_e, tile_r, tile_nv):
    n_e, cap, d = rows.shape
    f = w_o.shape[1]
    tf = _hidden_tile(f, 1792)
    n_f = f // tf
    tm = MOE_TM
    n_t = tile_e.shape[0]

    def jv(t, j, nv):
        return jnp.where(nv[t] > 0, j, 0)

    return pl.pallas_call(
        functools.partial(_experts_kernel, n_f=n_f),
        grid_spec=pltpu.PrefetchScalarGridSpec(
            num_scalar_prefetch=3,
            grid=(n_t, n_f),
            in_specs=[pl.BlockSpec((1, tm, d), lambda t, j, te, tr, nv: (te[t], tr[t], 0)),
                      pl.BlockSpec((1, d, tf), lambda t, j, te, tr, nv: (te[t], 0, jv(t, j, nv))),
                      pl.BlockSpec((1, d, tf), lambda t, j, te, tr, nv: (te[t], 0, jv(t, j, nv) + n_f)),
                      pl.BlockSpec((1, tf, d), lambda t, j, te, tr, nv: (te[t], jv(t, j, nv), 0))],
            out_specs=pl.BlockSpec((1, tm, d), lambda t, j, te, tr, nv: (te[t], tr[t], 0)),
            scratch_shapes=[pltpu.VMEM((tm, d), BF16), pltpu.VMEM((tm, d), F32)]),
        out_shape=jax.ShapeDtypeStruct((n_e, cap, d), BF16),
        compiler_params=_cparams(("arbitrary", "arbitrary")),
        name="moe_experts",
    )(tile_e, tile_r, tile_nv, rows, w_i, w_i, w_o)


def _combine_kernel(meta_ref, y_ref, route_ref, x_ref, gf_ref, fg_ref, o_ref, yl_scr, sem, *, n_e, n_b, final):
    b = pl.program_id(0)
    tb = x_ref.shape[0]
    lr = yl_scr.shape[1]
    slot = b % 2

    def fetch(blk, slt, wait):
        loc = 0
        for e in range(n_e):
            n = meta_ref[blk, n_e + e]
            _run_copies(n, y_ref.at[e], meta_ref[blk, e], yl_scr.at[slt], loc, sem.at[slt], wait)
            loc = loc + n
        return loc

    @pl.when(b == 0)
    def _():
        fetch(0, 0, False)

    @pl.when(b + 1 < n_b)
    def _():
        fetch(b + 1, 1 - slot, False)

    loc = fetch(b, slot, True)
    row = lax.broadcasted_iota(jnp.int32, (lr, yl_scr.shape[2]), 0)
    yl = jnp.where(row < loc, yl_scr[slot].astype(F32), 0.0).astype(BF16)
    route = route_ref[...]
    p1 = route[:, 0:1].astype(jnp.int32)
    p2 = route[:, 1:2].astype(jnp.int32)
    rr = lax.broadcasted_iota(jnp.int32, (tb, lr), 1)
    pw = jnp.where(rr == p1, route[:, 2:3], 0.0) + jnp.where(rr == p2, route[:, 3:4], 0.0)
    mix = jnp.dot(pw.astype(BF16), yl, preferred_element_type=F32)
    o_ref[...] = _finish(x_ref, gf_ref, fg_ref, mix, final)


def _combine(meta, y, route, x, gate_f, fg, n_e, tps, final):
    m, d = x.shape
    tb = MOE_TB
    lr = _moe_local_rows(n_e)
    return pl.pallas_call(
        functools.partial(_combine_kernel, n_e=n_e, n_b=m // tb, final=final),
        grid_spec=pltpu.PrefetchScalarGridSpec(
            num_scalar_prefetch=1,
            grid=(m // tb,),
            in_specs=[pl.BlockSpec(memory_space=pl.ANY),
                      pl.BlockSpec((tb, LANES), lambda i, mt: (i, 0)),
                      pl.BlockSpec((tb, d), lambda i, mt: (i, 0)),
                      _mod_spec(gate_f, tb, tps),
                      pl.BlockSpec((1, d), lambda i, mt: (0, 0))],
            out_specs=pl.BlockSpec((tb, d), lambda i, mt: (i, 0)),
            scratch_shapes=[pltpu.VMEM((2, lr, d), BF16), pltpu.SemaphoreType.DMA((2,))]),
        out_shape=jax.ShapeDtypeStruct((m, d), F32),
        compiler_params=_cparams(("arbitrary",)),
        name="moe_combine",
    )(meta, y, route, x, gate_f, fg)


def _moe_sparse(streams, g, fg, w_r, w_i, w_o, final):
    n_e = w_o.shape[0]
    d = streams[0]["x"].shape[1]
    m_all = sum(s["x"].shape[0] for s in streams)
    n_b_all = m_all // MOE_TB
    worst_e = m_all + n_b_all * (MOE_GRAN - 1)
    worst = TOP_K * m_all + n_b_all * n_e * (MOE_GRAN - 1)
    cap = (-(-worst_e // MOE_TM) + 1) * MOE_TM
    n_t = -(-worst // MOE_TM) + n_e
    start = jnp.zeros((n_e,), jnp.int32)
    rows = None
    routed = []
    for s in streams:
        route, meta, start, rows = _route(s["x"], g, s["scale"], s["shift"], w_r, start, rows, n_e, cap, s["tps"])
        routed.append((route, meta))
    fill = start
    tiles_e = (fill + MOE_TM - 1) // MOE_TM
    first = jnp.cumsum(tiles_e) - tiles_e
    t = jnp.arange(n_t, dtype=jnp.int32)
    e_of = jnp.sum((t[:, None] >= (first + tiles_e)[None, :]).astype(jnp.int32), axis=1)
    valid = e_of < n_e
    e_cl = jnp.minimum(e_of, n_e - 1)
    r_of = t - first[e_cl]
    nv = jnp.where(valid, jnp.clip(fill[e_cl] - r_of * MOE_TM, 0, MOE_TM), 0).astype(jnp.int32)
    tile_e = jnp.where(valid, e_cl, n_e - 1).astype(jnp.int32)
    tile_r = jnp.where(valid, r_of, cap // MOE_TM - 1).astype(jnp.int32)
    y = _experts(rows, w_i, w_o, tile_e, tile_r, nv)
    return [_combine(meta, y, route, s["x"], s["gate_f"], fg, n_e, s["tps"], final)
            for s, (route, meta) in zip(streams, routed)]


def _prep_weights(w_in, w_branch, w_out, ffn_w_in, ffn_w_out, router_w, moe_w_in, moe_w_out):
    depth = w_in.shape[0]
    w_main = jnp.concatenate([w_in[..., OFF_GATE:], w_in[..., OFF_DN_QKV:OFF_DN_Z], w_in[..., OFF_AQ:OFF_GLU],
                              w_in[..., OFF_GLU:OFF_DN_QKV], w_in[..., OFF_DN_Z:OFF_DN_A]], axis=-1).astype(BF16)
    w_ab = jnp.pad(w_in[..., OFF_DN_A:OFF_GATE], ((0, 0), (0, 0), (0, LANES - 2 * DN_HEADS))).astype(BF16)
    w_r = jnp.pad(router_w, ((0, 0), (0, 0), (0, LANES - router_w.shape[-1])))
    return dict(w_main=w_main, w_ab=w_ab, w_branch=w_branch.astype(BF16), w_out=w_out.astype(BF16),
                ffn_w_in=ffn_w_in.astype(BF16), ffn_w_out=ffn_w_out.astype(BF16), w_r=w_r,
                moe_w_in=moe_w_in.astype(BF16), moe_w_out=moe_w_out.astype(BF16), depth=depth)


class _Stream:
    def __init__(self, x3, mod, caches):
        self.n_seq, self.seq_len, self.d = x3.shape
        self.m = self.n_seq * self.seq_len
        self.x = x3.reshape(self.m, self.d)
        self.mod = mod
        self.caches = caches
        self.decode = caches is not None
        self.states = ([], [], [], [], [])

    def tile(self, want):
        return self.m if self.decode else want

    def tps(self, t):
        return 1 if self.decode else self.seq_len // t

    def modv(self, l, idx, t):
        v = self.mod[l][:, idx]
        if self.decode:
            return jnp.repeat(v, self.seq_len, axis=0).reshape(self.m // t, t, self.d)
        return v.reshape(self.n_seq, 1, self.d)


def _mixers(s, l, pw, p):
    n_seq, seq_len, m, decode, caches = s.n_seq, s.seq_len, s.m, s.decode, s.caches
    x = s.x
    tm, tm_big = s.tile(512), s.tile(1024)
    zmain, zab = _in_proj(x, p["norm_g"][l, 0:1], s.modv(l, 1, tm_big), s.modv(l, 0, tm_big),
                          pw["w_main"][l], pw["w_ab"][l], tm_big, s.tps(tm_big))
    if decode:
        cache_k, cache_v = caches[0][l], caches[1][l]
        nr = cache_k.shape[1]
        tab = _bias_table(p["rel_bias"][l], PAST_LEN, seq_len, PAST_LEN - nr, nr + seq_len)
        att = _attn_decode(zmain, cache_k.reshape(n_seq, nr, ATT_W), cache_v.reshape(n_seq, nr, ATT_W),
                           tab[:, :, :nr], tab[:, :, nr:], n_seq, seq_len)
        keep = seq_len
    else:
        qb, tq = 2 * CHUNK, N_BACK * CHUNK
        tab = _bias_table(p["rel_bias"][l], tq, qb, 0, tq + qb)
        att = _attn_prompt(zmain, tab, n_seq, seq_len)
        keep = min(N_BACK * CHUNK, seq_len)
    z3 = zmain.reshape(n_seq, seq_len, Z_W)
    k_rows = z3[:, seq_len - keep:, Z_K:Z_K + ATT_W].astype(F32).reshape(n_seq, keep, ATT_HEADS, ATT_HEAD_DIM)
    v_rows = z3[:, seq_len - keep:, Z_V:Z_V + ATT_W].astype(F32).reshape(n_seq, keep, ATT_HEADS, ATT_HEAD_DIM)
    if decode:
        cbuf = jnp.pad(caches[2][l], ((0, 0), (HALO - (CONV_WIDTH - 1), 0), (0, 0)))
        t_rows, sub = seq_len, seq_len
    else:
        cbuf = jnp.zeros((n_seq, HALO, CONV_CH), F32)
        t_rows, sub = 512, 64
    cw = jnp.pad(p["conv_w"][l], ((0, HALO - CONV_WIDTH), (0, 0)))
    cv, new_cbuf = _conv_module(zmain, cbuf, cw, p["conv_b"][l][None], p["conv_ln_g"][l][None],
                                p["conv_ln_b"][l][None], n_seq, seq_len, t_rows, sub)
    if decode:
        dbuf = jnp.pad(caches[3][l], ((0, 0), (DN_HALO - (SHORT_CONV - 1), 0), (0, 0)))
        s0 = caches[4][l]
    else:
        dbuf = jnp.zeros((n_seq, DN_HALO, DN_CONV_CH), F32)
        s0 = jnp.zeros((n_seq, DN_HEADS, DN_DK, DN_DV), F32)
    cs = seq_len if seq_len <= CHUNK else CHUNK
    lane_pad = (0, LANES - DN_HEADS)
    expa = jnp.pad(jnp.exp(p["dn_A_log"][l]), lane_pad)[None]
    dtb = jnp.pad(p["dn_dt_bias"][l], lane_pad)[None]
    dn, new_dbuf, s_new = _deltanet(z3, zab.reshape(n_seq, seq_len, LANES), dbuf, s0, p["dn_conv_w"][l],
                                    expa, dtb, p["dn_norm_g"][l][None], cs, 4)
    s.x = _merge(att, cv, dn.reshape(m, DN_W), zmain, x, s.modv(l, 2, tm), pw["w_branch"][l], pw["w_out"][l],
                 tm, s.tps(tm))
    for lst, st in zip(s.states, (k_rows, v_rows, new_cbuf, new_dbuf, s_new)):
        lst.append(st)


def _channel_mixer(streams, l, pw, p, final):
    g = p["norm_g"][l, 1:2]
    fg = p["final_norm_g"][None]
    if l % 2 == 0:
        for s in streams:
            t = s.tile(512)
            s.x = _ffn(s.x, g, s.modv(l, 4, t), s.modv(l, 3, t), s.modv(l, 5, t), fg, pw["ffn_w_in"][l // 2],
                       pw["ffn_w_out"][l // 2], t, s.tps(t), final)
    else:
        t = MOE_TB
        parts = [dict(x=s.x, scale=s.modv(l, 4, t), shift=s.modv(l, 3, t), gate_f=s.modv(l, 5, t),
                      tps=s.tps(t)) for s in streams]
        for s, xn in zip(streams, _moe_sparse(parts, g, fg, pw["w_r"][l // 2], pw["moe_w_in"][l // 2],
                                              pw["moe_w_out"][l // 2], final)):
            s.x = xn


def kernel(x_prompt, x_sample, c_prompt, c_sample, cache_attn_k, cache_attn_v, state_conv, state_dn_conv, state_dn,
           w_ada, b_ada, norm_g, w_in, rel_bias, conv_w, conv_b, conv_ln_g, conv_ln_b, dn_conv_w, dn_A_log,
           dn_dt_bias, dn_norm_g, w_branch, w_out, ffn_w_in, ffn_w_out, router_w, moe_w_in, moe_w_out,
           final_norm_g):
    depth = w_in.shape[0]
    d = x_prompt.shape[-1]
    n_p = c_prompt.shape[0]
    p = dict(norm_g=norm_g, rel_bias=rel_bias, conv_w=conv_w, conv_b=conv_b, conv_ln_g=conv_ln_g,
             conv_ln_b=conv_ln_b, dn_conv_w=dn_conv_w, dn_A_log=dn_A_log, dn_dt_bias=dn_dt_bias,
             dn_norm_g=dn_norm_g, final_norm_g=final_norm_g)
    pw = _prep_weights(w_in, w_branch, w_out, ffn_w_in, ffn_w_out, router_w, moe_w_in, moe_w_out)
    mod = _ada(jnp.concatenate([c_prompt, c_sample], axis=0), w_ada, b_ada)
    mod = mod.reshape(depth, -1, 6, d)
    streams = [_Stream(x_prompt, mod[:, :n_p], None),
               _Stream(x_sample, mod[:, n_p:], (cache_attn_k, cache_attn_v, state_conv, state_dn_conv, state_dn))]
    for l in range(depth):
        for s in streams:
            _mixers(s, l, pw, p)
        _channel_mixer(streams, l, pw, p, final=l == depth - 1)
    ys = tuple(s.x.reshape(s.n_seq, s.seq_len, d) for s in streams)
    return ys + tuple(jnp.stack(st) for s in streams for st in s.states)
```

```python
import functools

import numpy as np
import jax
import jax.numpy as jnp
from jax import lax
from jax.experimental import pallas as pl
from jax.experimental.pallas import tpu as pltpu

F32 = jnp.float32
BF16 = jnp.bfloat16

PAST_LEN = 4096
CHUNK = 64
N_BACK = 8
ATT_HEADS = 8
ATT_HEAD_DIM = 64
ATT_W = ATT_HEADS * ATT_HEAD_DIM
REL_CLIP = 128
CONV_CH = 512
CONV_WIDTH = 31
DN_HEADS = 4
DN_DK = 128
DN_DV = 128
DN_QK_W = DN_HEADS * DN_DK
DN_W = DN_HEADS * DN_DV
DN_CONV_CH = 2 * DN_QK_W + DN_W
SHORT_CONV = 4
N_BRANCH = 3
BRANCH_W = 512
TOP_K = 2
NORM_EPS = 1e-6
NEG_BIG = -1e30

OFF_AQ = 0
OFF_GLU = 3 * ATT_W
OFF_DN_QKV = OFF_GLU + 2 * CONV_CH
OFF_DN_Z = OFF_DN_QKV + DN_CONV_CH
OFF_DN_A = OFF_DN_Z + DN_W
OFF_GATE = OFF_DN_A + 2 * DN_HEADS

Z_GATE = 0
Z_DNQKV = 3072
Z_Q = 4608
Z_K = 5120
Z_V = 5632
Z_U1 = 6144
Z_U2 = 6656
Z_DZ = 7168
Z_W = 7680
LANES = 128
SUBLANES = 8
HALO = 32
DN_HALO = 8

VMEM_LIMIT = 56 * 1024 * 1024


def _cparams(sem):
    return pltpu.CompilerParams(dimension_semantics=sem, vmem_limit_bytes=VMEM_LIMIT)


def _sigmoid(x):
    return 1.0 / (1.0 + jnp.exp(-x))


def _silu(x):
    return x * _sigmoid(x)


def _mod_norm(x, g, scale, shift):
    ms = jnp.mean(x * x, axis=-1, keepdims=True)
    return (x * lax.rsqrt(ms + NORM_EPS) * g) * (1.0 + scale) + shift


def _ada_kernel(c_ref, w_ref, b_ref, o_ref):
    s = _silu(c_ref[...])
    o_ref[0] = jnp.dot(s.astype(BF16), w_ref[0].astype(BF16), preferred_element_type=F32) + b_ref[0]


def _ada(c_all, w_ada, b_ada):
    depth, d, n6 = w_ada.shape
    n = c_all.shape[0]
    tn = 1024
    return pl.pallas_call(
        _ada_kernel,
        grid=(depth, n6 // tn),
        in_specs=[pl.BlockSpec((n, d), lambda l, j: (0, 0)),
                  pl.BlockSpec((1, d, tn), lambda l, j: (l, 0, j)),
                  pl.BlockSpec((1, 1, tn), lambda l, j: (l, 0, j))],
        out_specs=pl.BlockSpec((1, n, tn), lambda l, j: (l, 0, j)),
        out_shape=jax.ShapeDtypeStruct((depth, n, n6), F32),
        compiler_params=_cparams(("parallel", "parallel")),
        name="ada_mod",
    )(c_all, w_ada, b_ada.reshape(depth, 1, n6))


def _in_proj_kernel(x_ref, g_ref, sc_ref, sh_ref, w_ref, wab_ref, z_ref, zab_ref, h_scr):
    @pl.when(pl.program_id(1) == 0)
    def _():
        h = _mod_norm(x_ref[...], g_ref[...], sc_ref[0], sh_ref[0]).astype(BF16)
        h_scr[...] = h
        zab_ref[...] = jnp.dot(h, wab_ref[...], preferred_element_type=F32)

    z_ref[...] = jnp.dot(h_scr[...], w_ref[...], preferred_element_type=F32).astype(BF16)


def _mod_spec(mod, tm, tps):
    r = mod.shape[1]
    if r == 1:
        return pl.BlockSpec((1, 1, mod.shape[2]), lambda i, *_: (i // tps, 0, 0))
    return pl.BlockSpec((1, r, mod.shape[2]), lambda i, *_: (i, 0, 0))


def _in_proj(x, g, scale, shift, w_main, w_ab, tm, tps):
    m, d = x.shape
    tn = 2560
    nt = Z_W // tn
    return pl.pallas_call(
        _in_proj_kernel,
        grid=(m // tm, nt),
        in_specs=[pl.BlockSpec((tm, d), lambda i, j: (i, 0)),
                  pl.BlockSpec((1, d), lambda i, j: (0, 0)),
                  _mod_spec(scale, tm, tps), _mod_spec(shift, tm, tps),
                  pl.BlockSpec((d, tn), lambda i, j: (0, j)),
                  pl.BlockSpec((d, LANES), lambda i, j: (0, 0))],
        out_specs=[pl.BlockSpec((tm, tn), lambda i, j: (i, j)),
                   pl.BlockSpec((tm, LANES), lambda i, j: (i, 0))],
        out_shape=[jax.ShapeDtypeStruct((m, Z_W), BF16), jax.ShapeDtypeStruct((m, LANES), F32)],
        scratch_shapes=[pltpu.VMEM((tm, d), BF16)],
        compiler_params=_cparams(("parallel", "arbitrary")),
        name="in_proj",
    )(x, g, scale, shift, w_main, w_ab)


def _softmax_pv(parts, vs):
    m = parts[0].max(axis=-1, keepdims=True)
    for s in parts[1:]:
        m = jnp.maximum(m, s.max(axis=-1, keepdims=True))
    num = None
    den = None
    for s, v in zip(parts, vs):
        p = jnp.exp(s - m)
        l = p.sum(axis=-1, keepdims=True)
        o = jnp.dot(p.astype(BF16), v, preferred_element_type=F32)
        num = o if num is None else num + o
        den = l if den is None else den + l
    return num / den


def _qk(q, k):
    return lax.dot_general(q, k, (((1,), (1,)), ((), ())), preferred_element_type=F32)


def _attn_prompt_kernel(q_ref, kp_ref, kc_ref, vp_ref, vc_ref, tab_ref, o_ref, *, tps, tq, qb, win):
    first = (pl.program_id(0) % tps == 0).astype(F32)
    kwin = jnp.concatenate([kp_ref[...], kc_ref[...]], axis=0)
    vwin = jnp.concatenate([vp_ref[...], vc_ref[...]], axis=0)
    rowid = lax.broadcasted_iota(jnp.int32, (1, 2 * tq), 1)
    neg = jnp.where(rowid < tq, first * NEG_BIG, 0.0)
    lo = lax.broadcasted_iota(jnp.int32, (qb, LANES), 1) < ATT_HEAD_DIM
    scale = ATT_HEAD_DIM ** -0.5
    pairs = range(ATT_HEADS // 2)
    for b in range(tq // qb):
        negw = neg[:, b * qb:b * qb + win]
        rows = slice(b * qb, (b + 1) * qb)
        wrows = slice(b * qb, b * qb + win)
        s = []
        for hp in pairs:
            q = q_ref[rows, hp * LANES:(hp + 1) * LANES] * scale
            zero = jnp.zeros_like(q)
            q2 = jnp.concatenate([jnp.where(lo, q, zero), jnp.where(lo, zero, q)], axis=0)
            s.append(_qk(q2, kwin[wrows, hp * LANES:(hp + 1) * LANES]) + tab_ref[hp] + negw)
        p = [jnp.exp(s[hp] - s[hp].max(axis=-1, keepdims=True)) for hp in pairs]
        den = [p[hp].sum(axis=-1, keepdims=True) for hp in pairs]
        num = [jnp.dot(p[hp].astype(BF16), vwin[wrows, hp * LANES:(hp + 1) * LANES], preferred_element_type=F32)
               for hp in pairs]
        for hp in pairs:
            o2 = num[hp] / den[hp]
            o_ref[rows, hp * LANES:(hp + 1) * LANES] = jnp.where(lo, o2[:qb], o2[qb:]).astype(BF16)


def _attn_prompt(zmain, tab, n_seq, seq_len):
    m = zmain.shape[0]
    tq = N_BACK * CHUNK
    qb = 2 * CHUNK
    win = tq + qb
    tps = seq_len // tq
    cq, ck, cv = Z_Q // ATT_W, Z_K // ATT_W, Z_V // ATT_W

    def prev(i):
        return jnp.where(i % tps == 0, i, i - 1)

    return pl.pallas_call(
        functools.partial(_attn_prompt_kernel, tps=tps, tq=tq, qb=qb, win=win),
        grid=(m // tq,),
        in_specs=[pl.BlockSpec((tq, ATT_W), lambda i: (i, cq)),
                  pl.BlockSpec((tq, ATT_W), lambda i: (prev(i), ck)),
                  pl.BlockSpec((tq, ATT_W), lambda i: (i, ck)),
                  pl.BlockSpec((tq, ATT_W), lambda i: (prev(i), cv)),
                  pl.BlockSpec((tq, ATT_W), lambda i: (i, cv)),
                  pl.BlockSpec((ATT_HEADS // 2, 2 * qb, win), lambda i: (0, 0, 0))],
        out_specs=pl.BlockSpec((tq, ATT_W), lambda i: (i, 0)),
        out_shape=jax.ShapeDtypeStruct((m, ATT_W), BF16),
        compiler_params=_cparams(("parallel",)),
        name="attn_prompt",
    )(zmain, zmain, zmain, zmain, zmain, tab.reshape(ATT_HEADS // 2, 2 * qb, win))


def _attn_decode_kernel(q_ref, kc_ref, kn_ref, vc_ref, vn_ref, tabc_ref, tabn_ref, o_ref):
    lq = q_ref.shape[0]
    kc = kc_ref[0].astype(BF16)
    vc = vc_ref[0].astype(BF16)
    lo = lax.broadcasted_iota(jnp.int32, (lq, LANES), 1) < ATT_HEAD_DIM
    scale = ATT_HEAD_DIM ** -0.5
    for hp in range(ATT_HEADS // 2):
        cols = slice(hp * LANES, (hp + 1) * LANES)
        q = q_ref[:, cols]
        outs = []
        for half in range(2):
            qm = jnp.where(lo if half == 0 else jnp.logical_not(lo), q, jnp.zeros_like(q))
            s1 = _qk(qm, kc[:, cols]) * scale + tabc_ref[2 * hp + half]
            s2 = _qk(qm, kn_ref[:, cols]) * scale + tabn_ref[2 * hp + half]
            outs.append(_softmax_pv([s1, s2], [vc[:, cols], vn_ref[:, cols]]))
        o_ref[:, cols] = jnp.where(lo, outs[0], outs[1]).astype(BF16)


def _attn_decode(zmain, cache_k, cache_v, tab_c, tab_n, n_seq, lq):
    m = zmain.shape[0]
    nr = cache_k.shape[1]
    cq, ck, cv = Z_Q // ATT_W, Z_K // ATT_W, Z_V // ATT_W
    return pl.pallas_call(
        _attn_decode_kernel,
        grid=(n_seq,),
        in_specs=[pl.BlockSpec((lq, ATT_W), lambda i: (i, cq)),
                  pl.BlockSpec((1, nr, ATT_W), lambda i: (i, 0, 0)),
                  pl.BlockSpec((lq, ATT_W), lambda i: (i, ck)),
                  pl.BlockSpec((1, nr, ATT_W), lambda i: (i, 0, 0)),
                  pl.BlockSpec((lq, ATT_W), lambda i: (i, cv)),
                  pl.BlockSpec((ATT_HEADS, lq, nr), lambda i: (0, 0, 0)),
                  pl.BlockSpec((ATT_HEADS, lq, lq), lambda i: (0, 0, 0))],
        out_specs=pl.BlockSpec((lq, ATT_W), lambda i: (i, 0)),
        out_shape=jax.ShapeDtypeStruct((m, ATT_W), BF16),
        compiler_params=_cparams(("parallel",)),
        name="attn_decode",
    )(zmain, cache_k, zmain, cache_v, zmain, tab_c, tab_n)


def _bias_table(rel_bias_l, qpos0, nq, kpos0, nk):
    qpos = qpos0 + np.arange(nq)
    kpos = kpos0 + np.arange(nk)
    qc = (qpos // CHUNK)[:, None]
    kc = (kpos // CHUNK)[None, :]
    ok = (kpos[None, :] >= 0) & (kc <= qc) & (kc >= qc - N_BACK)
    t_min = qpos0 - (kpos0 + nk - 1)
    t_max = qpos0 + nq - 1 - kpos0
    idx = np.clip(np.arange(t_max, t_min - 1, -1), -REL_CLIP, REL_CLIP) + REL_CLIP
    grev = rel_bias_l.astype(F32)[:, idx]
    n_h, l_all = grev.shape
    flat = jnp.tile(jnp.pad(grev, ((0, 0), (0, 1))), (1, nq))[:, :nq * l_all]
    tab = flat.reshape(n_h, nq, l_all)[:, :, nq - 1:nq - 1 + nk]
    return jnp.where(ok[None], tab, NEG_BIG)


def _conv_kernel(u1_ref, u2_ref, buf_ref, w_ref, cb_ref, lg_ref, lb_ref, o_ref, nb_ref, xp_scr, sh_scr,
                 *, t_rows, sub, n_t):
    _conv_tile(pl.program_id(1) == 0, u1_ref, u2_ref, buf_ref, w_ref, cb_ref, lg_ref, lb_ref, o_ref, nb_ref,
               xp_scr, sh_scr, t_rows, sub)


def _conv_tile(first, u1_ref, u2_ref, buf_ref, w_ref, cb_ref, lg_ref, lb_ref, o_ref, nb_ref, xp_scr, sh_scr,
               t_rows, sub):
    @pl.when(first)
    def _():
        xp_scr[0:HALO, :] = buf_ref[0]

    u1 = u1_ref[...].astype(F32)
    u2 = u2_ref[...].astype(F32)
    xp_scr[HALO:HALO + t_rows, :] = u1 * _sigmoid(u2)
    off = HALO - (CONV_WIDTH - 1)
    n_sh = HALO + t_rows - SUBLANES
    for r in range(1, SUBLANES):
        sh_scr[r - 1, 0:n_sh, :] = xp_scr[r:r + n_sh, :]
    for r0 in range(0, t_rows, sub):
        acc = None
        for j in range(CONV_WIDTH):
            a, r = divmod(off + j, SUBLANES)
            src = xp_scr[r0 + a * SUBLANES:r0 + a * SUBLANES + sub, :] if r == 0 else \
                sh_scr[r - 1, r0 + a * SUBLANES:r0 + a * SUBLANES + sub, :]
            term = w_ref[j:j + 1, :] * src
            acc = term if acc is None else acc + term
        cv = acc + cb_ref[...]
        mu = jnp.mean(cv, axis=-1, keepdims=True)
        cen = cv - mu
        var = jnp.mean(cen * cen, axis=-1, keepdims=True)
        y = cen * lax.rsqrt(var + NORM_EPS) * lg_ref[...] + lb_ref[...]
        o_ref[r0:r0 + sub, :] = _silu(y).astype(BF16)

    nb_ref[0] = xp_scr[t_rows + off:t_rows + HALO, :]
    xp_scr[0:HALO, :] = xp_scr[t_rows:t_rows + HALO, :]


def _conv_scratch(t_rows):
    return [pltpu.VMEM((HALO + t_rows, CONV_CH), F32),
            pltpu.VMEM((SUBLANES - 1, HALO + t_rows - SUBLANES, CONV_CH), F32)]


def _conv_module(zmain, buf, w, cb, lg, lb, n_seq, seq_len, t_rows, sub):
    m = zmain.shape[0]
    n_t = seq_len // t_rows
    c1, c2 = Z_U1 // CONV_CH, Z_U2 // CONV_CH
    vec = pl.BlockSpec((1, CONV_CH), lambda n, t: (0, 0))
    return pl.pallas_call(
        functools.partial(_conv_kernel, t_rows=t_rows, sub=sub, n_t=n_t),
        grid=(n_seq, n_t),
        in_specs=[pl.BlockSpec((t_rows, CONV_CH), lambda n, t: (n * n_t + t, c1)),
                  pl.BlockSpec((t_rows, CONV_CH), lambda n, t: (n * n_t + t, c2)),
                  pl.BlockSpec((1, HALO, CONV_CH), lambda n, t: (n, 0, 0)),
                  pl.BlockSpec((HALO, CONV_CH), lambda n, t: (0, 0)),
                  vec, vec, vec],
        out_specs=[pl.BlockSpec((t_rows, CONV_CH), lambda n, t: (n * n_t + t, 0)),
                   pl.BlockSpec((1, CONV_WIDTH - 1, CONV_CH), lambda n, t: (n, 0, 0))],
        out_shape=[jax.ShapeDtypeStruct((m, CONV_CH), BF16),
                   jax.ShapeDtypeStruct((n_seq, CONV_WIDTH - 1, CONV_CH), F32)],
        scratch_shapes=_conv_scratch(t_rows),
        compiler_params=_cparams(("parallel", "arbitrary")),
        name="conv_module",
    )(zmain, zmain, buf, w, cb, lg, lb)


def _split3(x):
    hi = x.astype(BF16)
    r = x - hi.astype(F32)
    mid = r.astype(BF16)
    lo = (r - mid.astype(F32)).astype(BF16)
    return hi, mid, lo


def _dn_kernel(qkv_ref, dz_ref, ab_ref, buf_ref, s0_ref, cw_ref, expa_ref, dtb_ref, ng_ref,
               o_ref, nbuf_ref, sfin_ref, xp_scr, s_scr, *, nb, cs, n_c):
    c = pl.program_id(1)

    @pl.when(c == 0)
    def _():
        xp_scr[:, 0:DN_HALO, :] = buf_ref[...]
        xp_scr[:, DN_HALO:2 * DN_HALO, :] = jnp.zeros((nb, DN_HALO, DN_CONV_CH), F32)
        s_scr[...] = s0_ref[...]

    lg = cs.bit_length() - 1
    si = lax.broadcasted_iota(jnp.int32, (cs, cs), 0)
    sj = lax.broadcasted_iota(jnp.int32, (cs, cs), 1)
    shift = jnp.concatenate([jnp.where(sj == si - k, 1.0, 0.0) for k in range(SHORT_CONV - 1, 0, -1)],
                            axis=0).astype(BF16)
    wd = DN_HEADS * cs
    ii = lax.broadcasted_iota(jnp.int32, (cs, cs), 0)
    jj = lax.broadcasted_iota(jnp.int32, (cs, cs), 1)
    lincl = jnp.where(ii >= jj, 1.0, 0.0).astype(BF16)
    r = lax.broadcasted_iota(jnp.int32, (cs, wd), 0)
    col = lax.broadcasted_iota(jnp.int32, (cs, wd), 1)
    jl = col & (cs - 1)
    hid = col >> lg
    incl_s = r >= jl
    strict_s = r > jl
    diag_s = r == jl
    rb = lax.broadcasted_iota(jnp.int32, (wd, wd), 0)
    cb = lax.broadcasted_iota(jnp.int32, (wd, wd), 1)
    same_head = (rb >> lg) == (cb >> lg)
    bd_mask = jnp.where(same_head, 1.0, 0.0).astype(BF16)
    kb_mask = jnp.where((lax.broadcasted_iota(jnp.int32, (wd, DN_QK_W), 0) >> lg) == (
        lax.broadcasted_iota(jnp.int32, (wd, DN_QK_W), 1) // DN_DK), 1.0, 0.0).astype(BF16)

    def level_sel(row, colm, sh):
        bi = row >> sh
        return jnp.where((bi & 1) == 1, bi - 1, -1) == (colm >> sh)

    level_mask = [jnp.where(same_head & level_sel(rb & (cs - 1), cb & (cs - 1), sh), 1.0, 0.0).astype(BF16)
                  for sh in range(1, lg)]

    def bd(m):
        return jnp.concatenate([m.astype(BF16)] * DN_HEADS, axis=0) * bd_mask

    off = DN_HALO - (SHORT_CONV - 1)
    heads = range(DN_HEADS)
    a_bd, d, qk, rhs, qg, kd, gl, dz = [], [], [], [], [], [], [], []
    for b in range(nb):
        xb = qkv_ref[b]
        xf = xb.astype(F32)
        sh3 = jnp.dot(shift, xb, preferred_element_type=F32)
        y = cw_ref[SHORT_CONV - 1:SHORT_CONV, :] * xf
        for j in range(SHORT_CONV - 1):
            y = y + cw_ref[j:j + 1, :] * sh3[j * cs:(j + 1) * cs]
        head = y[:DN_HALO]
        for j in range(SHORT_CONV - 1):
            head = head + cw_ref[j:j + 1, :] * xp_scr[b, off + j:off + j + DN_HALO, :]
        y = _silu(jnp.concatenate([head, y[DN_HALO:]], axis=0))
        xp_scr[b, 0:DN_HALO, :] = xf[cs - DN_HALO:]
        ab = ab_ref[b]
        xg = ab + dtb_ref[...]
        softplus = jnp.maximum(xg, 0.0) + jnp.log(1.0 + jnp.exp(-jnp.abs(xg)))
        gfull = -expa_ref[...] * softplus
        beta_full = _sigmoid(ab)
        gc3 = jnp.dot(lincl, jnp.concatenate(_split3(gfull), axis=1), preferred_element_type=F32)
        gcum = gc3[:, :LANES] + gc3[:, LANES:2 * LANES] + gc3[:, 2 * LANES:]
        dz.append(dz_ref[b].astype(F32))
        q, k, kb, kbg, vb, qg_b, kd_b, gl_b = [], [], [], [], [], [], [], []
        gcx = None
        for h in heads:
            qh = y[:, h * DN_DK:(h + 1) * DN_DK]
            kh = y[:, DN_QK_W + h * DN_DK:DN_QK_W + (h + 1) * DN_DK]
            vh = y[:, 2 * DN_QK_W + h * DN_DV:2 * DN_QK_W + (h + 1) * DN_DV]
            qh = qh * (lax.rsqrt(jnp.sum(qh * qh, axis=-1, keepdims=True) + 1e-6) * (DN_DK ** -0.5))
            kh = kh * lax.rsqrt(jnp.sum(kh * kh, axis=-1, keepdims=True) + 1e-6)
            gc = gcum[:, h:h + 1]
            beta = beta_full[:, DN_HEADS + h:DN_HEADS + h + 1]
            glast = gcum[cs - 1:cs, h:h + 1]
            eg = jnp.exp(gc)
            gcx = gc if gcx is None else jnp.where(hid == h, gc, gcx)
            q.append(qh)
            k.append(kh)
            kb.append(kh * beta)
            kbg.append(kh * (beta * eg))
            vb.append(vh * beta)
            qg_b.append(qh * eg)
            kd_b.append(kh * jnp.exp(glast - gc))
            gl_b.append(jnp.exp(glast))
        gcx = jnp.broadcast_to(gcx, (cs, wd))
        grx = jnp.sum(jnp.where(diag_s, gcx, 0.0), axis=0, keepdims=True)
        decay = jnp.exp(jnp.where(incl_s, gcx - grx, NEG_BIG))
        x_all = jnp.concatenate([jnp.concatenate(kb, axis=1), jnp.concatenate(q, axis=1)], axis=0).astype(BF16)
        k_bd = jnp.concatenate([jnp.concatenate(k, axis=1).astype(BF16)] * DN_HEADS, axis=0) * kb_mask
        kk = _qk(x_all, k_bd)
        a = jnp.where(strict_s, kk[:cs] * decay, 0.0)
        a_bd.append(bd(a))
        d.append(jnp.where(diag_s, 1.0, 0.0) - jnp.where(level_sel(r, jl, 0), a, 0.0))
        qk.append(kk[cs:] * decay)
        rhs.append(jnp.concatenate([jnp.concatenate([kbg[h], vb[h]], axis=1) for h in heads],
                                   axis=0).astype(BF16))
        qg.append(qg_b)
        kd.append(kd_b)
        gl.append(gl_b)
    for lm in level_mask:
        for b in range(nb):
            x = jnp.dot(d[b].astype(BF16), a_bd[b] * lm, preferred_element_type=F32)
            d[b] = d[b] - jnp.dot(x.astype(BF16), bd(d[b]), preferred_element_type=F32)
    wu = [jnp.dot(bd(d[b]), rhs[b], preferred_element_type=F32) for b in range(nb)]
    s_old = [[s_scr[b, h] for h in heads] for b in range(nb)]
    wq = [[jnp.dot(jnp.concatenate([wu[b][h * cs:(h + 1) * cs, :DN_DK], qg[b][h]], axis=0).astype(BF16),
                   s_old[b][h].astype(BF16), preferred_element_type=F32) for h in heads] for b in range(nb)]
    v_new = [[wu[b][h * cs:(h + 1) * cs, DN_DK:] - wq[b][h][:cs] for h in heads] for b in range(nb)]
    o2 = [jnp.dot(bd(qk[b]), jnp.concatenate(v_new[b], axis=0).astype(BF16), preferred_element_type=F32)
          for b in range(nb)]
    for b in range(nb):
        for h in heads:
            o = wq[b][h][cs:] + o2[b][h * cs:(h + 1) * cs]
            s_scr[b, h] = s_old[b][h] * gl[b][h] + lax.dot_general(
                kd[b][h].astype(BF16), v_new[b][h].astype(BF16), (((0,), (0,)), ((), ())),
                preferred_element_type=F32)
            on = o * lax.rsqrt(jnp.mean(o * o, axis=-1, keepdims=True) + NORM_EPS) * ng_ref[...]
            dzh = dz[b][:, h * DN_DV:(h + 1) * DN_DV]
            o_ref[b, :, h * DN_DV:(h + 1) * DN_DV] = (on * _silu(dzh)).astype(BF16)

    @pl.when(c == n_c - 1)
    def _():
        nbuf_ref[...] = xp_scr[:, off:DN_HALO, :]
        sfin_ref[...] = s_scr[...]


def _deltanet(zmain3, zab3, buf, s0, cw, expa, dtb, ng, cs, nb):
    n_seq, seq_len, _ = zmain3.shape
    n_c = seq_len // cs
    cq, cz = Z_DNQKV // DN_CONV_CH, Z_DZ // DN_W
    row = pl.BlockSpec((1, LANES), lambda s, c: (0, 0))
    return pl.pallas_call(
        functools.partial(_dn_kernel, nb=nb, cs=cs, n_c=n_c),
        grid=(n_seq // nb, n_c),
        in_specs=[pl.BlockSpec((nb, cs, DN_CONV_CH), lambda s, c: (s, c, cq)),
                  pl.BlockSpec((nb, cs, DN_W), lambda s, c: (s, c, cz)),
                  pl.BlockSpec((nb, cs, LANES), lambda s, c: (s, c, 0)),
                  pl.BlockSpec((nb, DN_HALO, DN_CONV_CH), lambda s, c: (s, 0, 0)),
                  pl.BlockSpec((nb, DN_HEADS, DN_DK, DN_DV), lambda s, c: (s, 0, 0, 0)),
                  pl.BlockSpec((SHORT_CONV, DN_CONV_CH), lambda s, c: (0, 0)),
                  row, row, row],
        out_specs=[pl.BlockSpec((nb, cs, DN_W), lambda s, c: (s, c, 0)),
                   pl.BlockSpec((nb, SHORT_CONV - 1, DN_CONV_CH), lambda s, c: (s, 0, 0)),
                   pl.BlockSpec((nb, DN_HEADS, DN_DK, DN_DV), lambda s, c: (s, 0, 0, 0))],
        out_shape=[jax.ShapeDtypeStruct((n_seq, seq_len, DN_W), BF16),
                   jax.ShapeDtypeStruct((n_seq, SHORT_CONV - 1, DN_CONV_CH), F32),
                   jax.ShapeDtypeStruct((n_seq, DN_HEADS, DN_DK, DN_DV), F32)],
        scratch_shapes=[pltpu.VMEM((nb, 2 * DN_HALO, DN_CONV_CH), F32),
                        pltpu.VMEM((nb, DN_HEADS, DN_DK, DN_DV), F32)],
        compiler_params=_cparams(("parallel", "arbitrary")),
        name="deltanet",
    )(zmain3, zmain3, zab3, buf, s0, cw, expa, dtb, ng)


def _merge_body(branches, gates, x_ref, gm_ref, wb_ref, wo_ref, o_ref):
    acc = None
    for b, (br, gr) in enumerate(zip(branches, gates)):
        yb = jnp.dot(br[...], wb_ref[b], preferred_element_type=F32)
        t = _sigmoid(gr[...].astype(F32)) * yb
        acc = t if acc is None else acc + t
    out = jnp.dot(acc.astype(BF16), wo_ref[...], preferred_element_type=F32)
    o_ref[...] = x_ref[...] + gm_ref[0] * out


def _merge_kernel(att_ref, cv_ref, dn_ref, g0_ref, g1_ref, g2_ref, x_ref, gm_ref, wb_ref, wo_ref, o_ref):
    _merge_body((att_ref, cv_ref, dn_ref), (g0_ref, g1_ref, g2_ref), x_ref, gm_ref, wb_ref, wo_ref, o_ref)


def _merge(att, cv, dn, zmain, x, gate_m, wb, wo, tm, tps):
    m, d = x.shape
    br = pl.BlockSpec((tm, BRANCH_W), lambda i: (i, 0))
    return pl.pallas_call(
        _merge_kernel,
        grid=(m // tm,),
        in_specs=[br, br, br,
                  pl.BlockSpec((tm, d), lambda i: (i, 0)),
                  pl.BlockSpec((tm, d), lambda i: (i, 1)),
                  pl.BlockSpec((tm, d), lambda i: (i, 2)),
                  pl.BlockSpec((tm, d), lambda i: (i, 0)),
                  _mod_spec(gate_m, tm, tps),
                  pl.BlockSpec((N_BRANCH, BRANCH_W, d), lambda i: (0, 0, 0)),
                  pl.BlockSpec((d, d), lambda i: (0, 0))],
        out_specs=pl.BlockSpec((tm, d), lambda i: (i, 0)),
        out_shape=jax.ShapeDtypeStruct((m, d), F32),
        compiler_params=_cparams(("parallel",)),
        name="merge",
    )(att, cv, dn, zmain, zmain, zmain, x, gate_m, wb, wo)


def _merge_conv_kernel(att_ref, u1_ref, u2_ref, dn_ref, g0_ref, g1_ref, g2_ref, x_ref, gm_ref, wb_ref, wo_ref,
                       buf_ref, cw_ref, cb_ref, lg_ref, lb_ref, o_ref, nb_ref, xp_scr, sh_scr, cv_scr,
                       *, tps, sub):
    _conv_tile(pl.program_id(0) % tps == 0, u1_ref, u2_ref, buf_ref, cw_ref, cb_ref, lg_ref, lb_ref, cv_scr, nb_ref,
               xp_scr, sh_scr, cv_scr.shape[0], sub)
    _merge_body((att_ref, cv_scr, dn_ref), (g0_ref, g1_ref, g2_ref), x_ref, gm_ref, wb_ref, wo_ref, o_ref)


def _merge_conv(att, dn, zmain, x, gate_m, wb, wo, cbuf, cw, cb, lg, lb, tm, tps, sub):
    m, d = x.shape
    n_seq = m // (tm * tps)
    c1, c2 = Z_U1 // CONV_CH, Z_U2 // CONV_CH
    br = pl.BlockSpec((tm, BRANCH_W), lambda i: (i, 0))
    vec = pl.BlockSpec((1, CONV_CH), lambda i: (0, 0))
    return pl.pallas_call(
        functools.partial(_merge_conv_kernel, tps=tps, sub=sub),
        grid=(m // tm,),
        in_specs=[br,
                  pl.BlockSpec((tm, CONV_CH), lambda i: (i, c1)),
                  pl.BlockSpec((tm, CONV_CH), lambda i: (i, c2)),
                  br,
                  pl.BlockSpec((tm, d), lambda i: (i, 0)),
                  pl.BlockSpec((tm, d), lambda i: (i, 1)),
                  pl.BlockSpec((tm, d), lambda i: (i, 2)),
                  pl.BlockSpec((tm, d), lambda i: (i, 0)),
                  _mod_spec(gate_m, tm, tps),
                  pl.BlockSpec((N_BRANCH, BRANCH_W, d), lambda i: (0, 0, 0)),
                  pl.BlockSpec((d, d), lambda i: (0, 0)),
                  pl.BlockSpec((1, HALO, CONV_CH), lambda i: (i // tps, 0, 0)),
                  pl.BlockSpec((HALO, CONV_CH), lambda i: (0, 0)),
                  vec, vec, vec],
        out_specs=[pl.BlockSpec((tm, d), lambda i: (i, 0)),
                   pl.BlockSpec((1, CONV_WIDTH - 1, CONV_CH), lambda i: (i // tps, 0, 0))],
        out_shape=[jax.ShapeDtypeStruct((m, d), F32),
                   jax.ShapeDtypeStruct((n_seq, CONV_WIDTH - 1, CONV_CH), F32)],
        scratch_shapes=_conv_scratch(tm) + [pltpu.VMEM((tm, CONV_CH), BF16)],
        compiler_params=_cparams(("arbitrary",)),
        name="merge_conv",
    )(att, zmain, zmain, dn, zmain, zmain, zmain, x, gate_m, wb, wo, cbuf, cw, cb, lg, lb)


def _finish(x_ref, gf_ref, fg_ref, acc, final):
    xn = x_ref[...] + gf_ref[0] * acc
    if final:
        ms = jnp.mean(xn * xn, axis=-1, keepdims=True)
        xn = xn * lax.rsqrt(ms + NORM_EPS) * fg_ref[...]
    return xn


def _ffn_kernel(x_ref, g_ref, sc_ref, sh_ref, gf_ref, fg_ref, wa_ref, wb_ref, wo_ref, o_ref, h_scr, acc_scr,
                *, n_f, final):
    j = pl.program_id(1)

    @pl.when(j == 0)
    def _():
        h_scr[...] = _mod_norm(x_ref[...], g_ref[...], sc_ref[0], sh_ref[0]).astype(BF16)
        acc_scr[...] = jnp.zeros_like(acc_scr)

    h = h_scr[...]
    a = jnp.dot(h, wa_ref[...], preferred_element_type=F32)
    b = jnp.dot(h, wb_ref[...], preferred_element_type=F32)
    acc_scr[...] += jnp.dot((_silu(a) * b).astype(BF16), wo_ref[...], preferred_element_type=F32)

    @pl.when(j == n_f - 1)
    def _():
        o_ref[...] = _finish(x_ref, gf_ref, fg_ref, acc_scr[...], final)


def _hidden_tile(f, limit):
    return max(t for t in range(LANES, min(f, limit) + 1, LANES) if f % t == 0)


def _ffn(x, g, scale, shift, gate_f, fg, w_i, w_o, tm, tps, final):
    m, d = x.shape
    f = w_o.shape[0]
    tf = _hidden_tile(f, 1408)
    n_f = f // tf
    return pl.pallas_call(
        functools.partial(_ffn_kernel, n_f=n_f, final=final),
        grid=(m // tm, n_f),
        in_specs=[pl.BlockSpec((tm, d), lambda i, j: (i, 0)),
                  pl.BlockSpec((1, d), lambda i, j: (0, 0)),
                  _mod_spec(scale, tm, tps), _mod_spec(shift, tm, tps), _mod_spec(gate_f, tm, tps),
                  pl.BlockSpec((1, d), lambda i, j: (0, 0)),
                  pl.BlockSpec((d, tf), lambda i, j: (0, j)),
                  pl.BlockSpec((d, tf), lambda i, j: (0, j + n_f)),
                  pl.BlockSpec((tf, d), lambda i, j: (j, 0))],
        out_specs=pl.BlockSpec((tm, d), lambda i, j: (i, 0)),
        out_shape=jax.ShapeDtypeStruct((m, d), F32),
        scratch_shapes=[pltpu.VMEM((tm, d), BF16), pltpu.VMEM((tm, d), F32)],
        compiler_params=_cparams(("parallel", "arbitrary")),
        name="ffn",
    )(x, g, scale, shift, gate_f, fg, w_i, w_i, w_o)


MOE_TB = 512
MOE_GRAN = 16
MOE_TM = 512
MOE_RUN_BITS = 6


def _moe_local_rows(n_e):
    return TOP_K * MOE_TB + n_e * MOE_GRAN


def _run_copies(n, src_ref, src0, dst_ref, dst0, sem, wait):
    off = 0
    for bit in reversed(range(MOE_RUN_BITS)):
        size = MOE_GRAN << bit
        take = (n & size) != 0
        cp = pltpu.make_async_copy(src_ref.at[pl.ds(pl.multiple_of(src0 + off, MOE_GRAN), size)],
                                   dst_ref.at[pl.ds(pl.multiple_of(dst0 + off, MOE_GRAN), size)], sem)

        @pl.when(take)
        def _():
            if wait:
                cp.wait()
            else:
                cp.start()

        off = off + jnp.where(take, size, 0)


def _route_kernel(start_ref, x_ref, g_ref, sc_ref, sh_ref, wr_ref, *rest, n_e, n_steps, nk, aliased):
    if aliased:
        rest = rest[1:]
    route_ref, meta_ref, fill_ref, rows_ref, sorted_scr, cur_scr, sem = rest
    step = pl.program_id(0)
    tb = x_ref.shape[0] // nk
    lr = sorted_scr.shape[1]
    blocks = range(nk)

    @pl.when(step == 0)
    def _():
        for e in range(n_e):
            cur_scr[e] = start_ref[e]

    def rows_of(ref0, k):
        return ref0 if ref0.shape[0] == 1 else ref0[k * tb:(k + 1) * tb]

    sc, sh = sc_ref[0], sh_ref[0]
    h = [_mod_norm(x_ref[k * tb:(k + 1) * tb, :], g_ref[...], rows_of(sc, k), rows_of(sh, k)) for k in blocks]
    w_hi, w_mid, _ = _split3(wr_ref[...])
    logits = []
    for k in blocks:
        h_hi, h_mid, _ = _split3(h[k])
        logits.append(jnp.dot(h_hi, w_hi, preferred_element_type=F32)
                      + (jnp.dot(h_hi, w_mid, preferred_element_type=F32)
                         + jnp.dot(h_mid, w_hi, preferred_element_type=F32)))
    lane = lax.broadcasted_iota(jnp.int32, (tb, LANES), 1)
    ti = lax.broadcasted_iota(jnp.int32, (tb, tb), 0)
    tj = lax.broadcasted_iota(jnp.int32, (tb, tb), 1)
    earlier = jnp.where(ti > tj, 1.0, 0.0).astype(BF16)
    ei = lax.broadcasted_iota(jnp.int32, (LANES, LANES), 0)
    ej = lax.broadcasted_iota(jnp.int32, (LANES, LANES), 1)
    before = jnp.where(ei < ej, 1.0, 0.0).astype(BF16)
    rr = lax.broadcasted_iota(jnp.int32, (tb, lr), 1)
    i1, i2, w1, w2, sel = [], [], [], [], []
    for k in blocks:
        lg = jnp.where(lane < n_e, logits[k], NEG_BIG)
        m1 = lg.max(axis=-1, keepdims=True)
        a1 = jnp.where(lg == m1, lane, LANES).min(axis=-1, keepdims=True)
        lg2 = jnp.where(lane == a1, NEG_BIG, lg)
        m2 = lg2.max(axis=-1, keepdims=True)
        a2 = jnp.where(lg2 == m2, lane, LANES).min(axis=-1, keepdims=True)
        e2 = jnp.exp(m2 - m1)
        i1.append(a1)
        i2.append(a2)
        w1.append(1.0 / (1.0 + e2))
        w2.append(e2 / (1.0 + e2))
        sel.append(jnp.where(lane == a1, 1.0, 0.0) + jnp.where(lane == a2, 1.0, 0.0))
    rank = [jnp.dot(earlier, sel[k].astype(BF16), preferred_element_type=F32) for k in blocks]
    cpad = [jnp.floor((jnp.sum(sel[k], axis=0, keepdims=True) + (MOE_GRAN - 1)) * (1.0 / MOE_GRAN)) * MOE_GRAN
            for k in blocks]
    loc = [jnp.dot(jnp.broadcast_to(cpad[k], (SUBLANES, LANES)).astype(BF16), before,
                   preferred_element_type=F32)[0:1] for k in blocks]
    for k in blocks:
        pos = loc[k] + rank[k]
        p1 = jnp.sum(jnp.where(lane == i1[k], pos, 0.0), axis=-1, keepdims=True)
        p2 = jnp.sum(jnp.where(lane == i2[k], pos, 0.0), axis=-1, keepdims=True)
        route_ref[k * tb:(k + 1) * tb, :] = jnp.where(
            lane == 0, p1, jnp.where(lane == 1, p2, jnp.where(lane == 2, w1[k], jnp.where(lane == 3, w2[k], 0.0))))
        perm = jnp.where((rr == p1.astype(jnp.int32)) | (rr == p2.astype(jnp.int32)), 1.0, 0.0).astype(BF16)
        sorted_scr[k] = lax.dot_general(perm, h[k].astype(BF16), (((0,), (0,)), ((), ())),
                                        preferred_element_type=F32).astype(BF16)
    runs = []
    for k in blocks:
        cp_i = cpad[k].astype(jnp.int32)
        loc_i = loc[k].astype(jnp.int32)
        blk = step * nk + k
        for e in range(n_e):
            n = cp_i[0, e]
            dst0 = cur_scr[e]
            meta_ref[blk, e] = dst0
            meta_ref[blk, n_e + e] = n
            cur_scr[e] = dst0 + n
            runs.append((k, e, n, loc_i[0, e], dst0))
    for wait in (False, True):
        for k, e, n, src0, dst0 in runs:
            _run_copies(n, sorted_scr.at[k], src0, rows_ref.at[e], dst0, sem, wait)

    @pl.when(step == n_steps - 1)
    def _():
        for e in range(n_e):
            fill_ref[e] = cur_scr[e]


def _route(x, g, scale, shift, w_r, start, rows_in, n_e, cap, tps):
    m, d = x.shape
    tb = MOE_TB
    n_b = m // tb
    nk = 2 if (n_b % 2 == 0 and tps % 2 == 0) else 1
    lr = _moe_local_rows(n_e)
    aliased = rows_in is not None
    smem = pl.BlockSpec(memory_space=pltpu.SMEM)
    hbm = pl.BlockSpec(memory_space=pl.ANY)
    in_specs = [smem,
                pl.BlockSpec((nk * tb, d), lambda i: (i, 0)),
                pl.BlockSpec((1, d), lambda i: (0, 0)),
                _mod_spec(scale, nk * tb, tps // nk), _mod_spec(shift, nk * tb, tps // nk),
                pl.BlockSpec((d, LANES), lambda i: (0, 0))]
    args = [start, x, g, scale, shift, w_r]
    if aliased:
        in_specs.append(hbm)
        args.append(rows_in)
    return pl.pallas_call(
        functools.partial(_route_kernel, n_e=n_e, n_steps=n_b // nk, nk=nk, aliased=aliased),
        grid=(n_b // nk,),
        in_specs=in_specs,
        out_specs=[pl.BlockSpec((nk * tb, LANES), lambda i: (i, 0)), smem, smem, hbm],
        out_shape=[jax.ShapeDtypeStruct((m, LANES), F32),
                   jax.ShapeDtypeStruct((n_b, 2 * n_e), jnp.int32),
                   jax.ShapeDtypeStruct((n_e,), jnp.int32),
                   jax.ShapeDtypeStruct((n_e, cap, d), BF16)],
        scratch_shapes=[pltpu.VMEM((nk, lr, d), BF16), pltpu.SMEM((n_e,), jnp.int32), pltpu.SemaphoreType.DMA(())],
        input_output_aliases={len(args) - 1: 3} if aliased else {},
        compiler_params=_cparams(("arbitrary",)),
        name="moe_route",
    )(*args)


def _experts_kernel(te_ref, tr_ref, nv_ref, x_ref, wa_ref, wb_ref, wo_ref, y_ref, xs_scr, acc_scr, *, n_f):
    t = pl.program_id(0)
    j = pl.program_id(1)
    nv = nv_ref[t]

    @pl.when(nv > 0)
    def _():
        @pl.when(j == 0)
        def _():
            row = lax.broadcasted_iota(jnp.int32, xs_scr.shape, 0)
            xs_scr[...] = jnp.where(row < nv, x_ref[0].astype(F32), 0.0).astype(BF16)
            acc_scr[...] = jnp.zeros_like(acc_scr)

        xs = xs_scr[...]
        a = jnp.dot(xs, wa_ref[0], preferred_element_type=F32)
        b = jnp.dot(xs, wb_ref[0], preferred_element_type=F32)
        acc_scr[...] += jnp.dot((_silu(a) * b).astype(BF16), wo_ref[0], preferred_element_type=F32)

        @pl.when(j == n_f - 1)
        def _():
            y_ref[0] = acc_scr[...].astype(BF16)

    @pl.when((nv == 0) & (j == n_f - 1))
    def _():
        y_ref[0] = jnp.zeros(y_ref.shape[1:], BF16)


def _experts(rows, w_i, w_o, tile_e, tile_r, tile_nv):
    n_e, cap, d = rows.shape
    f = w_o.shape[1]
    tf = _hidden_tile(f, 1792)
    n_f = f // tf
    tm = MOE_TM
    n_t = tile_e.shape[0]

    def jv(t, j, nv):
        return jnp.where(nv[t] > 0, j, 0)

    return pl.pallas_call(
        functools.partial(_experts_kernel, n_f=n_f),
        grid_spec=pltpu.PrefetchScalarGridSpec(
            num_scalar_prefetch=3,
            grid=(n_t, n_f),
            in_specs=[pl.BlockSpec((1, tm, d), lambda t, j, te, tr, nv: (te[t], tr[t], 0)),
                      pl.BlockSpec((1, d, tf), lambda t, j, te, tr, nv: (te[t], 0, jv(t, j, nv))),
                      pl.BlockSpec((1, d, tf), lambda t, j, te, tr, nv: (te[t], 0, jv(t, j, nv) + n_f)),
                      pl.BlockSpec((1, tf, d), lambda t, j, te, tr, nv: (te[t], jv(t, j, nv), 0))],
            out_specs=pl.BlockSpec((1, tm, d), lambda t, j, te, tr, nv: (te[t], tr[t], 0)),
            scratch_shapes=[pltpu.VMEM((tm, d), BF16), pltpu.VMEM((tm, d), F32)]),
        out_shape=jax.ShapeDtypeStruct((n_e, cap, d), BF16),
        compiler_params=_cparams(("arbitrary", "arbitrary")),
        name="moe_experts",
    )(tile_e, tile_r, tile_nv, rows, w_i, w_i, w_o)


def _combine_kernel(meta_ref, y_ref, route_ref, x_ref, gf_ref, fg_ref, o_ref, yl_scr, sem, *, n_e, n_b, final):
    b = pl.program_id(0)
    tb = x_ref.shape[0]
    lr = yl_scr.shape[1]
    slot = b % 2

    def fetch(blk, slt, wait):
        loc = 0
        for e in range(n_e):
            n = meta_ref[blk, n_e + e]
            _run_copies(n, y_ref.at[e], meta_ref[blk, e], yl_scr.at[slt], loc, sem.at[slt], wait)
            loc = loc + n
        return loc

    @pl.when(b == 0)
    def _():
        fetch(0, 0, False)

    @pl.when(b + 1 < n_b)
    def _():
        fetch(b + 1, 1 - slot, False)

    loc = fetch(b, slot, True)
    row = lax.broadcasted_iota(jnp.int32, (lr, yl_scr.shape[2]), 0)
    yl = jnp.where(row < loc, yl_scr[slot].astype(F32), 0.0).astype(BF16)
    route = route_ref[...]
    p1 = route[:, 0:1].astype(jnp.int32)
    p2 = route[:, 1:2].astype(jnp.int32)
    rr = lax.broadcasted_iota(jnp.int32, (tb, lr), 1)
    pw = jnp.where(rr == p1, route[:, 2:3], 0.0) + jnp.where(rr == p2, route[:, 3:4], 0.0)
    mix = jnp.dot(pw.astype(BF16), yl, preferred_element_type=F32)
    o_ref[...] = _finish(x_ref, gf_ref, fg_ref, mix, final)


def _combine(meta, y, route, x, gate_f, fg, n_e, tps, final):
    m, d = x.shape
    tb = MOE_TB
    lr = _moe_local_rows(n_e)
    return pl.pallas_call(
        functools.partial(_combine_kernel, n_e=n_e, n_b=m // tb, final=final),
        grid_spec=pltpu.PrefetchScalarGridSpec(
            num_scalar_prefetch=1,
            grid=(m // tb,),
            in_specs=[pl.BlockSpec(memory_space=pl.ANY),
                      pl.BlockSpec((tb, LANES), lambda i, mt: (i, 0)),
                      pl.BlockSpec((tb, d), lambda i, mt: (i, 0)),
                      _mod_spec(gate_f, tb, tps),
                      pl.BlockSpec((1, d), lambda i, mt: (0, 0))],
            out_specs=pl.BlockSpec((tb, d), lambda i, mt: (i, 0)),
            scratch_shapes=[pltpu.VMEM((2, lr, d), BF16), pltpu.SemaphoreType.DMA((2,))]),
        out_shape=jax.ShapeDtypeStruct((m, d), F32),
        compiler_params=_cparams(("arbitrary",)),
        name="moe_combine",
    )(meta, y, route, x, gate_f, fg)


def _moe_sparse(streams, g, fg, w_r, w_i, w_o, final):
    n_e = w_o.shape[0]
    d = streams[0]["x"].shape[1]
    m_all = sum(s["x"].shape[0] for s in streams)
    n_b_all = m_all // MOE_TB
    worst_e = m_all + n_b_all * (MOE_GRAN - 1)
    worst = TOP_K * m_all + n_b_all * n_e * (MOE_GRAN - 1)
    cap = (-(-worst_e // MOE_TM) + 1) * MOE_TM
    n_t = -(-worst // MOE_TM) + n_e
    start = jnp.zeros((n_e,), jnp.int32)
    rows = None
    routed = []
    for s in streams:
        route, meta, start, rows = _route(s["x"], g, s["scale"], s["shift"], w_r, start, rows, n_e, cap, s["tps"])
        routed.append((route, meta))
    fill = start
    tiles_e = (fill + MOE_TM - 1) // MOE_TM
    first = jnp.cumsum(tiles_e) - tiles_e
    t = jnp.arange(n_t, dtype=jnp.int32)
    e_of = jnp.sum((t[:, None] >= (first + tiles_e)[None, :]).astype(jnp.int32), axis=1)
    valid = e_of < n_e
    e_cl = jnp.minimum(e_of, n_e - 1)
    r_of = t - first[e_cl]
    nv = jnp.where(valid, jnp.clip(fill[e_cl] - r_of * MOE_TM, 0, MOE_TM), 0).astype(jnp.int32)
    tile_e = jnp.where(valid, e_cl, n_e - 1).astype(jnp.int32)
    tile_r = jnp.where(valid, r_of, cap // MOE_TM - 1).astype(jnp.int32)
    y = _experts(rows, w_i, w_o, tile_e, tile_r, nv)
    return [_combine(meta, y, route, s["x"], s["gate_f"], fg, n_e, s["tps"], final)
            for s, (route, meta) in zip(streams, routed)]


def _prep_weights(w_in, w_branch, w_out, ffn_w_in, ffn_w_out, router_w, moe_w_in, moe_w_out):
    depth = w_in.shape[0]
    w_main = jnp.concatenate([w_in[..., OFF_GATE:], w_in[..., OFF_DN_QKV:OFF_DN_Z], w_in[..., OFF_AQ:OFF_GLU],
                              w_in[..., OFF_GLU:OFF_DN_QKV], w_in[..., OFF_DN_Z:OFF_DN_A]], axis=-1).astype(BF16)
    w_ab = jnp.pad(w_in[..., OFF_DN_A:OFF_GATE], ((0, 0), (0, 0), (0, LANES - 2 * DN_HEADS))).astype(BF16)
    w_r = jnp.pad(router_w, ((0, 0), (0, 0), (0, LANES - router_w.shape[-1])))
    return dict(w_main=w_main, w_ab=w_ab, w_branch=w_branch.astype(BF16), w_out=w_out.astype(BF16),
                ffn_w_in=ffn_w_in.astype(BF16), ffn_w_out=ffn_w_out.astype(BF16), w_r=w_r,
                moe_w_in=moe_w_in.astype(BF16), moe_w_out=moe_w_out.astype(BF16), depth=depth)


class _Stream:
    def __init__(self, x3, mod, caches):
        self.n_seq, self.seq_len, self.d = x3.shape
        self.m = self.n_seq * self.seq_len
        self.x = x3.reshape(self.m, self.d)
        self.mod = mod
        self.caches = caches
        self.decode = caches is not None
        self.states = ([], [], [], [], [])

    def tile(self, want):
        return self.m if self.decode else want

    def tps(self, t):
        return 1 if self.decode else self.seq_len // t

    def modv(self, l, idx, t):
        v = self.mod[l][:, idx]
        if self.decode:
            return jnp.repeat(v, self.seq_len, axis=0).reshape(self.m // t, t, self.d)
        return v.reshape(self.n_seq, 1, self.d)


def _mixers(s, l, pw, p):
    n_seq, seq_len, m, decode, caches = s.n_seq, s.seq_len, s.m, s.decode, s.caches
    x = s.x
    tm, tm_big = s.tile(512), s.tile(1024)
    zmain, zab = _in_proj(x, p["norm_g"][l, 0:1], s.modv(l, 1, tm_big), s.modv(l, 0, tm_big),
                          pw["w_main"][l], pw["w_ab"][l], tm_big, s.tps(tm_big))
    if decode:
        cache_k, cache_v = caches[0][l], caches[1][l]
        nr = cache_k.shape[1]
        tab = _bias_table(p["rel_bias"][l], PAST_LEN, seq_len, PAST_LEN - nr, nr + seq_len)
        att = _attn_decode(zmain, cache_k.reshape(n_seq, nr, ATT_W), cache_v.reshape(n_seq, nr, ATT_W),
                           tab[:, :, :nr], tab[:, :, nr:], n_seq, seq_len)
        keep = seq_len
    else:
        qb, tq = 2 * CHUNK, N_BACK * CHUNK
        tab = _bias_table(p["rel_bias"][l], tq, qb, 0, tq + qb)
        att = _attn_prompt(zmain, tab, n_seq, seq_len)
        keep = min(N_BACK * CHUNK, seq_len)
    z3 = zmain.reshape(n_seq, seq_len, Z_W)
    k_rows = z3[:, seq_len - keep:, Z_K:Z_K + ATT_W].astype(F32).reshape(n_seq, keep, ATT_HEADS, ATT_HEAD_DIM)
    v_rows = z3[:, seq_len - keep:, Z_V:Z_V + ATT_W].astype(F32).reshape(n_seq, keep, ATT_HEADS, ATT_HEAD_DIM)
    if decode:
        cbuf = jnp.pad(caches[2][l], ((0, 0), (HALO - (CONV_WIDTH - 1), 0), (0, 0)))
        t_rows, sub = seq_len, seq_len
    else:
        cbuf = jnp.zeros((n_seq, HALO, CONV_CH), F32)
        t_rows, sub = 512, 64
    cw = jnp.pad(p["conv_w"][l], ((0, HALO - CONV_WIDTH), (0, 0)))
    conv_args = (cbuf, cw, p["conv_b"][l][None], p["conv_ln_g"][l][None], p["conv_ln_b"][l][None])
    if decode:
        cv, new_cbuf = _conv_module(zmain, *conv_args, n_seq, seq_len, t_rows, sub)
    if decode:
        dbuf = jnp.pad(caches[3][l], ((0, 0), (DN_HALO - (SHORT_CONV - 1), 0), (0, 0)))
        s0 = caches[4][l]
    else:
        dbuf = jnp.zeros((n_seq, DN_HALO, DN_CONV_CH), F32)
        s0 = jnp.zeros((n_seq, DN_HEADS, DN_DK, DN_DV), F32)
    cs = seq_len if seq_len <= CHUNK else CHUNK
    lane_pad = (0, LANES - DN_HEADS)
    expa = jnp.pad(jnp.exp(p["dn_A_log"][l]), lane_pad)[None]
    dtb = jnp.pad(p["dn_dt_bias"][l], lane_pad)[None]
    dn, new_dbuf, s_new = _deltanet(z3, zab.reshape(n_seq, seq_len, LANES), dbuf, s0, p["dn_conv_w"][l],
                                    expa, dtb, p["dn_norm_g"][l][None], cs, 4)
    if decode:
        s.x = _merge(att, cv, dn.reshape(m, DN_W), zmain, x, s.modv(l, 2, tm), pw["w_branch"][l], pw["w_out"][l],
                     tm, s.tps(tm))
    else:
        s.x, new_cbuf = _merge_conv(att, dn.reshape(m, DN_W), zmain, x, s.modv(l, 2, tm), pw["w_branch"][l],
                                    pw["w_out"][l], *conv_args, tm, s.tps(tm), sub)
    for lst, st in zip(s.states, (k_rows, v_rows, new_cbuf, new_dbuf, s_new)):
        lst.append(st)


def _channel_mixer(streams, l, pw, p, final):
    g = p["norm_g"][l, 1:2]
    fg = p["final_norm_g"][None]
    if l % 2 == 0:
        for s in streams:
            t = s.tile(512)
            s.x = _ffn(s.x, g, s.modv(l, 4, t), s.modv(l, 3, t), s.modv(l, 5, t), fg, pw["ffn_w_in"][l // 2],
                       pw["ffn_w_out"][l // 2], t, s.tps(t), final)
    else:
        t = MOE_TB
        parts = [dict(x=s.x, scale=s.modv(l, 4, t), shift=s.modv(l, 3, t), gate_f=s.modv(l, 5, t),
                      tps=s.tps(t)) for s in streams]
        for s, xn in zip(streams, _moe_sparse(parts, g, fg, pw["w_r"][l // 2], pw["moe_w_in"][l // 2],
                                              pw["moe_w_out"][l // 2], final)):
            s.x = xn


def kernel(x_prompt, x_sample, c_prompt, c_sample, cache_attn_k, cache_attn_v, state_conv, state_dn_conv, state_dn,
           w_ada, b_ada, norm_g, w_in, rel_bias, conv_w, conv_b, conv_ln_g, conv_ln_b, dn_conv_w, dn_A_log,
           dn_dt_bias, dn_norm_g, w_branch, w_out, ffn_w_in, ffn_w_out, router_w, moe_w_in, moe_w_out,
           final_norm_g):
    depth = w_in.shape[0]
    d = x_prompt.shape[-1]
    n_p = c_prompt.shape[0]
    p = dict(norm_g=norm_g, rel_bias=rel_bias, conv_w=conv_w, conv_b=conv_b, conv_ln_g=conv_ln_g,
             conv_ln_b=conv_ln_b, dn_conv_w=dn_conv_w, dn_A_log=dn_A_log, dn_dt_bias=dn_dt_bias,
             dn_norm_g=dn_norm_g, final_norm_g=final_norm_g)
    pw = _prep_weights(w_in, w_branch, w_out, ffn_w_in, ffn_w_out, router_w, moe_w_in, moe_w_out)
    mod = _ada(jnp.concatenate([c_prompt, c_sample], axis=0), w_ada, b_ada)
    mod = mod.reshape(depth, -1, 6, d)
    streams = [_Stream(x_prompt, mod[:, :n_p], None),
               _Stream(x_sample, mod[:, n_p:], (cache_attn_k, cache_attn_v, state_conv, state_dn_conv, state_dn))]
    for l in range(depth):
        for s in streams:
            _mixers(s, l, pw, p)
        _channel_mixer(streams, l, pw, p, final=l == depth - 1)
    ys = tuple(s.x.reshape(s.n_seq, s.seq_len, d) for s in streams)
    return ys + tuple(jnp.stack(st) for s in streams for st in s.states)
```

```python
import functools

import numpy as np
import jax
import jax.numpy as jnp
from jax import lax
from jax.experimental import pallas as pl
from jax.experimental.pallas import tpu as pltpu

F32 = jnp.float32
BF16 = jnp.bfloat16

PAST_LEN = 4096
CHUNK = 64
N_BACK = 8
ATT_HEADS = 8
ATT_HEAD_DIM = 64
ATT_W = ATT_HEADS * ATT_HEAD_DIM
REL_CLIP = 128
CONV_CH = 512
CONV_WIDTH = 31
DN_HEADS = 4
DN_DK = 128
DN_DV = 128
DN_QK_W = DN_HEADS * DN_DK
DN_W = DN_HEADS * DN_DV
DN_CONV_CH = 2 * DN_QK_W + DN_W
SHORT_CONV = 4
N_BRANCH = 3
BRANCH_W = 512
TOP_K = 2
NORM_EPS = 1e-6
NEG_BIG = -1e30

OFF_AQ = 0
OFF_GLU = 3 * ATT_W
OFF_DN_QKV = OFF_GLU + 2 * CONV_CH
OFF_DN_Z = OFF_DN_QKV + DN_CONV_CH
OFF_DN_A = OFF_DN_Z + DN_W
OFF_GATE = OFF_DN_A + 2 * DN_HEADS

Z_GATE = 0
Z_DNQKV = 3072
Z_Q = 4608
Z_K = 5120
Z_V = 5632
Z_U1 = 6144
Z_U2 = 6656
Z_DZ = 7168
Z_W = 7680
LANES = 128
SUBLANES = 8
HALO = 32
DN_HALO = 8

VMEM_LIMIT = 56 * 1024 * 1024


def _cparams(sem):
    return pltpu.CompilerParams(dimension_semantics=sem, vmem_limit_bytes=VMEM_LIMIT)


def _sigmoid(x):
    return 1.0 / (1.0 + jnp.exp(-x))


def _silu(x):
    return x * _sigmoid(x)


def _mod_norm(x, g, scale, shift):
    ms = jnp.mean(x * x, axis=-1, keepdims=True)
    return (x * lax.rsqrt(ms + NORM_EPS) * g) * (1.0 + scale) + shift


def _ada_kernel(c_ref, w_ref, b_ref, o_ref):
    s = _silu(c_ref[...])
    o_ref[0] = jnp.dot(s.astype(BF16), w_ref[0].astype(BF16), preferred_element_type=F32) + b_ref[0]


def _ada(c_all, w_ada, b_ada):
    depth, d, n6 = w_ada.shape
    n = c_all.shape[0]
    tn = 1024
    return pl.pallas_call(
        _ada_kernel,
        grid=(depth, n6 // tn),
        in_specs=[pl.BlockSpec((n, d), lambda l, j: (0, 0)),
                  pl.BlockSpec((1, d, tn), lambda l, j: (l, 0, j)),
                  pl.BlockSpec((1, 1, tn), lambda l, j: (l, 0, j))],
        out_specs=pl.BlockSpec((1, n, tn), lambda l, j: (l, 0, j)),
        out_shape=jax.ShapeDtypeStruct((depth, n, n6), F32),
        compiler_params=_cparams(("parallel", "parallel")),
        name="ada_mod",
    )(c_all, w_ada, b_ada.reshape(depth, 1, n6))


def _in_proj_kernel(x_ref, g_ref, sc_ref, sh_ref, w_ref, wab_ref, z_ref, zab_ref, *, tn):
    h = _mod_norm(x_ref[...], g_ref[...], sc_ref[0], sh_ref[0]).astype(BF16)
    zab_ref[...] = jnp.dot(h, wab_ref[...], preferred_element_type=F32)
    for c0 in range(0, z_ref.shape[1], tn):
        z_ref[:, c0:c0 + tn] = jnp.dot(h, w_ref[:, c0:c0 + tn], preferred_element_type=F32).astype(BF16)


def _mod_spec(mod, tm, tps):
    r = mod.shape[1]
    if r == 1:
        return pl.BlockSpec((1, 1, mod.shape[2]), lambda i, *_: (i // tps, 0, 0))
    return pl.BlockSpec((1, r, mod.shape[2]), lambda i, *_: (i, 0, 0))


def _in_proj(x, g, scale, shift, w_main, w_ab, tm, tps):
    m, d = x.shape
    once = pl.Buffered(1)
    return pl.pallas_call(
        functools.partial(_in_proj_kernel, tn=1536),
        grid=(m // tm,),
        in_specs=[pl.BlockSpec((tm, d), lambda i: (i, 0)),
                  pl.BlockSpec((1, d), lambda i: (0, 0)),
                  _mod_spec(scale, tm, tps), _mod_spec(shift, tm, tps),
                  pl.BlockSpec((d, Z_W), lambda i: (0, 0), pipeline_mode=once),
                  pl.BlockSpec((d, LANES), lambda i: (0, 0), pipeline_mode=once)],
        out_specs=[pl.BlockSpec((tm, Z_W), lambda i: (i, 0)),
                   pl.BlockSpec((tm, LANES), lambda i: (i, 0))],
        out_shape=[jax.ShapeDtypeStruct((m, Z_W), BF16), jax.ShapeDtypeStruct((m, LANES), F32)],
        compiler_params=_cparams(("parallel",)),
        name="in_proj",
    )(x, g, scale, shift, w_main, w_ab)


def _softmax_pv(parts, vs):
    m = parts[0].max(axis=-1, keepdims=True)
    for s in parts[1:]:
        m = jnp.maximum(m, s.max(axis=-1, keepdims=True))
    num = None
    den = None
    for s, v in zip(parts, vs):
        p = jnp.exp(s - m)
        l = p.sum(axis=-1, keepdims=True)
        o = jnp.dot(p.astype(BF16), v, preferred_element_type=F32)
        num = o if num is None else num + o
        den = l if den is None else den + l
    return num / den


def _qk(q, k):
    return lax.dot_general(q, k, (((1,), (1,)), ((), ())), preferred_element_type=F32)


def _attn_prompt_kernel(q_ref, kp_ref, kc_ref, vp_ref, vc_ref, tab_ref, o_ref, *, tps, tq, qb, win):
    first = (pl.program_id(0) % tps == 0).astype(F32)
    kwin = jnp.concatenate([kp_ref[...], kc_ref[...]], axis=0)
    vwin = jnp.concatenate([vp_ref[...], vc_ref[...]], axis=0)
    rowid = lax.broadcasted_iota(jnp.int32, (1, 2 * tq), 1)
    neg = jnp.where(rowid < tq, first * NEG_BIG, 0.0)
    lo = lax.broadcasted_iota(jnp.int32, (qb, LANES), 1) < ATT_HEAD_DIM
    scale = ATT_HEAD_DIM ** -0.5
    pairs = range(ATT_HEADS // 2)
    for b in range(tq // qb):
        negw = neg[:, b * qb:b * qb + win]
        rows = slice(b * qb, (b + 1) * qb)
        wrows = slice(b * qb, b * qb + win)
        s = []
        for hp in pairs:
            q = q_ref[rows, hp * LANES:(hp + 1) * LANES] * scale
            zero = jnp.zeros_like(q)
            q2 = jnp.concatenate([jnp.where(lo, q, zero), jnp.where(lo, zero, q)], axis=0)
            s.append(_qk(q2, kwin[wrows, hp * LANES:(hp + 1) * LANES]) + tab_ref[hp] + negw)
        p = [jnp.exp(s[hp] - s[hp].max(axis=-1, keepdims=True)) for hp in pairs]
        den = [p[hp].sum(axis=-1, keepdims=True) for hp in pairs]
        num = [jnp.dot(p[hp].astype(BF16), vwin[wrows, hp * LANES:(hp + 1) * LANES], preferred_element_type=F32)
               for hp in pairs]
        for hp in pairs:
            o2 = num[hp] / den[hp]
            o_ref[rows, hp * LANES:(hp + 1) * LANES] = jnp.where(lo, o2[:qb], o2[qb:]).astype(BF16)


def _attn_prompt(zmain, tab, n_seq, seq_len):
    m = zmain.shape[0]
    tq = N_BACK * CHUNK
    qb = 2 * CHUNK
    win = tq + qb
    tps = seq_len // tq
    cq, ck, cv = Z_Q // ATT_W, Z_K // ATT_W, Z_V // ATT_W

    def prev(i):
        return jnp.where(i % tps == 0, i, i - 1)

    return pl.pallas_call(
        functools.partial(_attn_prompt_kernel, tps=tps, tq=tq, qb=qb, win=win),
        grid=(m // tq,),
        in_specs=[pl.BlockSpec((tq, ATT_W), lambda i: (i, cq)),
                  pl.BlockSpec((tq, ATT_W), lambda i: (prev(i), ck)),
                  pl.BlockSpec((tq, ATT_W), lambda i: (i, ck)),
                  pl.BlockSpec((tq, ATT_W), lambda i: (prev(i), cv)),
                  pl.BlockSpec((tq, ATT_W), lambda i: (i, cv)),
                  pl.BlockSpec((ATT_HEADS // 2, 2 * qb, win), lambda i: (0, 0, 0))],
        out_specs=pl.BlockSpec((tq, ATT_W), lambda i: (i, 0)),
        out_shape=jax.ShapeDtypeStruct((m, ATT_W), BF16),
        compiler_params=_cparams(("parallel",)),
        name="attn_prompt",
    )(zmain, zmain, zmain, zmain, zmain, tab.reshape(ATT_HEADS // 2, 2 * qb, win))


def _attn_decode_kernel(q_ref, kc_ref, kn_ref, vc_ref, vn_ref, tabc_ref, tabn_ref, o_ref):
    lq = q_ref.shape[0]
    kc = kc_ref[0].astype(BF16)
    vc = vc_ref[0].astype(BF16)
    lo = lax.broadcasted_iota(jnp.int32, (lq, LANES), 1) < ATT_HEAD_DIM
    scale = ATT_HEAD_DIM ** -0.5
    for hp in range(ATT_HEADS // 2):
        cols = slice(hp * LANES, (hp + 1) * LANES)
        q = q_ref[:, cols]
        outs = []
        for half in range(2):
            qm = jnp.where(lo if half == 0 else jnp.logical_not(lo), q, jnp.zeros_like(q))
            s1 = _qk(qm, kc[:, cols]) * scale + tabc_ref[2 * hp + half]
            s2 = _qk(qm, kn_ref[:, cols]) * scale + tabn_ref[2 * hp + half]
            outs.append(_softmax_pv([s1, s2], [vc[:, cols], vn_ref[:, cols]]))
        o_ref[:, cols] = jnp.where(lo, outs[0], outs[1]).astype(BF16)


def _attn_decode(zmain, cache_k, cache_v, tab_c, tab_n, n_seq, lq):
    m = zmain.shape[0]
    nr = cache_k.shape[1]
    cq, ck, cv = Z_Q // ATT_W, Z_K // ATT_W, Z_V // ATT_W
    return pl.pallas_call(
        _attn_decode_kernel,
        grid=(n_seq,),
        in_specs=[pl.BlockSpec((lq, ATT_W), lambda i: (i, cq)),
                  pl.BlockSpec((1, nr, ATT_W), lambda i: (i, 0, 0)),
                  pl.BlockSpec((lq, ATT_W), lambda i: (i, ck)),
                  pl.BlockSpec((1, nr, ATT_W), lambda i: (i, 0, 0)),
                  pl.BlockSpec((lq, ATT_W), lambda i: (i, cv)),
                  pl.BlockSpec((ATT_HEADS, lq, nr), lambda i: (0, 0, 0)),
                  pl.BlockSpec((ATT_HEADS, lq, lq), lambda i: (0, 0, 0))],
        out_specs=pl.BlockSpec((lq, ATT_W), lambda i: (i, 0)),
        out_shape=jax.ShapeDtypeStruct((m, ATT_W), BF16),
        compiler_params=_cparams(("parallel",)),
        name="attn_decode",
    )(zmain, cache_k, zmain, cache_v, zmain, tab_c, tab_n)


def _bias_table(rel_bias_l, qpos0, nq, kpos0, nk):
    qpos = qpos0 + np.arange(nq)
    kpos = kpos0 + np.arange(nk)
    qc = (qpos // CHUNK)[:, None]
    kc = (kpos // CHUNK)[None, :]
    ok = (kpos[None, :] >= 0) & (kc <= qc) & (kc >= qc - N_BACK)
    t_min = qpos0 - (kpos0 + nk - 1)
    t_max = qpos0 + nq - 1 - kpos0
    idx = np.clip(np.arange(t_max, t_min - 1, -1), -REL_CLIP, REL_CLIP) + REL_CLIP
    grev = rel_bias_l.astype(F32)[:, idx]
    n_h, l_all = grev.shape
    flat = jnp.tile(jnp.pad(grev, ((0, 0), (0, 1))), (1, nq))[:, :nq * l_all]
    tab = flat.reshape(n_h, nq, l_all)[:, :, nq - 1:nq - 1 + nk]
    return jnp.where(ok[None], tab, NEG_BIG)


def _conv_kernel(u1_ref, u2_ref, buf_ref, w_ref, cb_ref, lg_ref, lb_ref, o_ref, nb_ref, xp_scr, sh_scr,
                 *, t_rows, sub, n_t):
    _conv_tile(pl.program_id(1) == 0, u1_ref, u2_ref, buf_ref, w_ref, cb_ref, lg_ref, lb_ref, o_ref, nb_ref,
               xp_scr, sh_scr, t_rows, sub)


def _conv_tile(first, u1_ref, u2_ref, buf_ref, w_ref, cb_ref, lg_ref, lb_ref, o_ref, nb_ref, xp_scr, sh_scr,
               t_rows, sub):
    @pl.when(first)
    def _():
        xp_scr[0:HALO, :] = buf_ref[0]

    u1 = u1_ref[...].astype(F32)
    u2 = u2_ref[...].astype(F32)
    xp_scr[HALO:HALO + t_rows, :] = u1 * _sigmoid(u2)
    off = HALO - (CONV_WIDTH - 1)
    n_sh = HALO + t_rows - SUBLANES
    for r in range(1, SUBLANES):
        sh_scr[r - 1, 0:n_sh, :] = xp_scr[r:r + n_sh, :]
    for r0 in range(0, t_rows, sub):
        acc = None
        for j in range(CONV_WIDTH):
            a, r = divmod(off + j, SUBLANES)
            src = xp_scr[r0 + a * SUBLANES:r0 + a * SUBLANES + sub, :] if r == 0 else \
                sh_scr[r - 1, r0 + a * SUBLANES:r0 + a * SUBLANES + sub, :]
            term = w_ref[j:j + 1, :] * src
            acc = term if acc is None else acc + term
        cv = acc + cb_ref[...]
        mu = jnp.mean(cv, axis=-1, keepdims=True)
        cen = cv - mu
        var = jnp.mean(cen * cen, axis=-1, keepdims=True)
        y = cen * lax.rsqrt(var + NORM_EPS) * lg_ref[...] + lb_ref[...]
        o_ref[r0:r0 + sub, :] = _silu(y).astype(BF16)

    nb_ref[0] = xp_scr[t_rows + off:t_rows + HALO, :]
    xp_scr[0:HALO, :] = xp_scr[t_rows:t_rows + HALO, :]


def _conv_scratch(t_rows):
    return [pltpu.VMEM((HALO + t_rows, CONV_CH), F32),
            pltpu.VMEM((SUBLANES - 1, HALO + t_rows - SUBLANES, CONV_CH), F32)]


def _conv_module(zmain, buf, w, cb, lg, lb, n_seq, seq_len, t_rows, sub):
    m = zmain.shape[0]
    n_t = seq_len // t_rows
    c1, c2 = Z_U1 // CONV_CH, Z_U2 // CONV_CH
    vec = pl.BlockSpec((1, CONV_CH), lambda n, t: (0, 0))
    return pl.pallas_call(
        functools.partial(_conv_kernel, t_rows=t_rows, sub=sub, n_t=n_t),
        grid=(n_seq, n_t),
        in_specs=[pl.BlockSpec((t_rows, CONV_CH), lambda n, t: (n * n_t + t, c1)),
                  pl.BlockSpec((t_rows, CONV_CH), lambda n, t: (n * n_t + t, c2)),
                  pl.BlockSpec((1, HALO, CONV_CH), lambda n, t: (n, 0, 0)),
                  pl.BlockSpec((HALO, CONV_CH), lambda n, t: (0, 0)),
                  vec, vec, vec],
        out_specs=[pl.BlockSpec((t_rows, CONV_CH), lambda n, t: (n * n_t + t, 0)),
                   pl.BlockSpec((1, CONV_WIDTH - 1, CONV_CH), lambda n, t: (n, 0, 0))],
        out_shape=[jax.ShapeDtypeStruct((m, CONV_CH), BF16),
                   jax.ShapeDtypeStruct((n_seq, CONV_WIDTH - 1, CONV_CH), F32)],
        scratch_shapes=_conv_scratch(t_rows),
        compiler_params=_cparams(("parallel", "arbitrary")),
        name="conv_module",
    )(zmain, zmain, buf, w, cb, lg, lb)


def _split3(x):
    hi = x.astype(BF16)
    r = x - hi.astype(F32)
    mid = r.astype(BF16)
    lo = (r - mid.astype(F32)).astype(BF16)
    return hi, mid, lo


def _dn_kernel(qkv_ref, dz_ref, ab_ref, buf_ref, s0_ref, cw_ref, expa_ref, dtb_ref, ng_ref,
               o_ref, nbuf_ref, sfin_ref, xp_scr, s_scr, *, nb, cs, n_c):
    c = pl.program_id(1)

    @pl.when(c == 0)
    def _():
        xp_scr[:, 0:DN_HALO, :] = buf_ref[...]
        xp_scr[:, DN_HALO:2 * DN_HALO, :] = jnp.zeros((nb, DN_HALO, DN_CONV_CH), F32)
        s_scr[...] = s0_ref[...]

    lg = cs.bit_length() - 1
    si = lax.broadcasted_iota(jnp.int32, (cs, cs), 0)
    sj = lax.broadcasted_iota(jnp.int32, (cs, cs), 1)
    shift = jnp.concatenate([jnp.where(sj == si - k, 1.0, 0.0) for k in range(SHORT_CONV - 1, 0, -1)],
                            axis=0).astype(BF16)
    wd = DN_HEADS * cs
    ii = lax.broadcasted_iota(jnp.int32, (cs, cs), 0)
    jj = lax.broadcasted_iota(jnp.int32, (cs, cs), 1)
    lincl = jnp.where(ii >= jj, 1.0, 0.0).astype(BF16)
    r = lax.broadcasted_iota(jnp.int32, (cs, wd), 0)
    col = lax.broadcasted_iota(jnp.int32, (cs, wd), 1)
    jl = col & (cs - 1)
    hid = col >> lg
    incl_s = r >= jl
    strict_s = r > jl
    diag_s = r == jl
    rb = lax.broadcasted_iota(jnp.int32, (wd, wd), 0)
    cb = lax.broadcasted_iota(jnp.int32, (wd, wd), 1)
    same_head = (rb >> lg) == (cb >> lg)
    bd_mask = jnp.where(same_head, 1.0, 0.0).astype(BF16)
    kb_mask = jnp.where((lax.broadcasted_iota(jnp.int32, (wd, DN_QK_W), 0) >> lg) == (
        lax.broadcasted_iota(jnp.int32, (wd, DN_QK_W), 1) // DN_DK), 1.0, 0.0).astype(BF16)

    def level_sel(row, colm, sh):
        bi = row >> sh
        return jnp.where((bi & 1) == 1, bi - 1, -1) == (colm >> sh)

    level_mask = [jnp.where(same_head & level_sel(rb & (cs - 1), cb & (cs - 1), sh), 1.0, 0.0).astype(BF16)
                  for sh in range(1, lg)]

    def bd(m):
        return jnp.concatenate([m.astype(BF16)] * DN_HEADS, axis=0) * bd_mask

    off = DN_HALO - (SHORT_CONV - 1)
    heads = range(DN_HEADS)
    a_bd, d, qk, rhs, qg, kd, gl, dz = [], [], [], [], [], [], [], []
    for b in range(nb):
        xb = qkv_ref[b]
        xf = xb.astype(F32)
        sh3 = jnp.dot(shift, xb, preferred_element_type=F32)
        y = cw_ref[SHORT_CONV - 1:SHORT_CONV, :] * xf
        for j in range(SHORT_CONV - 1):
            y = y + cw_ref[j:j + 1, :] * sh3[j * cs:(j + 1) * cs]
        head = y[:DN_HALO]
        for j in range(SHORT_CONV - 1):
            head = head + cw_ref[j:j + 1, :] * xp_scr[b, off + j:off + j + DN_HALO, :]
        y = _silu(jnp.concatenate([head, y[DN_HALO:]], axis=0))
        xp_scr[b, 0:DN_HALO, :] = xf[cs - DN_HALO:]
        ab = ab_ref[b]
        xg = ab + dtb_ref[...]
        softplus = jnp.maximum(xg, 0.0) + jnp.log(1.0 + jnp.exp(-jnp.abs(xg)))
        gfull = -expa_ref[...] * softplus
        beta_full = _sigmoid(ab)
        gc3 = jnp.dot(lincl, jnp.concatenate(_split3(gfull), axis=1), preferred_element_type=F32)
        gcum = gc3[:, :LANES] + gc3[:, LANES:2 * LANES] + gc3[:, 2 * LANES:]
        dz.append(dz_ref[b].astype(F32))
        q, k, kb, kbg, vb, qg_b, kd_b, gl_b = [], [], [], [], [], [], [], []
        gcx = None
        for h in heads:
            qh = y[:, h * DN_DK:(h + 1) * DN_DK]
            kh = y[:, DN_QK_W + h * DN_DK:DN_QK_W + (h + 1) * DN_DK]
            vh = y[:, 2 * DN_QK_W + h * DN_DV:2 * DN_QK_W + (h + 1) * DN_DV]
            qh = qh * (lax.rsqrt(jnp.sum(qh * qh, axis=-1, keepdims=True) + 1e-6) * (DN_DK ** -0.5))
            kh = kh * lax.rsqrt(jnp.sum(kh * kh, axis=-1, keepdims=True) + 1e-6)
            gc = gcum[:, h:h + 1]
            beta = beta_full[:, DN_HEADS + h:DN_HEADS + h + 1]
            glast = gcum[cs - 1:cs, h:h + 1]
            eg = jnp.exp(gc)
            gcx = gc if gcx is None else jnp.where(hid == h, gc, gcx)
            q.append(qh)
            k.append(kh)
            kb.append(kh * beta)
            kbg.append(kh * (beta * eg))
            vb.append(vh * beta)
            qg_b.append(qh * eg)
            kd_b.append(kh * jnp.exp(glast - gc))
            gl_b.append(jnp.exp(glast))
        gcx = jnp.broadcast_to(gcx, (cs, wd))
        grx = jnp.sum(jnp.where(diag_s, gcx, 0.0), axis=0, keepdims=True)
        decay = jnp.exp(jnp.where(incl_s, gcx - grx, NEG_BIG))
        x_all = jnp.concatenate([jnp.concatenate(kb, axis=1), jnp.concatenate(q, axis=1)], axis=0).astype(BF16)
        k_bd = jnp.concatenate([jnp.concatenate(k, axis=1).astype(BF16)] * DN_HEADS, axis=0) * kb_mask
        kk = _qk(x_all, k_bd)
        a = jnp.where(strict_s, kk[:cs] * decay, 0.0)
        a_bd.append(bd(a))
        d.append(jnp.where(diag_s, 1.0, 0.0) - jnp.where(level_sel(r, jl, 0), a, 0.0))
        qk.append(kk[cs:] * decay)
        rhs.append(jnp.concatenate([jnp.concatenate([kbg[h], vb[h]], axis=1) for h in heads],
                                   axis=0).astype(BF16))
        qg.append(qg_b)
        kd.append(kd_b)
        gl.append(gl_b)
    for lm in level_mask:
        for b in range(nb):
            x = jnp.dot(d[b].astype(BF16), a_bd[b] * lm, preferred_element_type=F32)
            d[b] = d[b] - jnp.dot(x.astype(BF16), bd(d[b]), preferred_element_type=F32)
    wu = [jnp.dot(bd(d[b]), rhs[b], preferred_element_type=F32) for b in range(nb)]
    s_old = [[s_scr[b, h] for h in heads] for b in range(nb)]
    wq = [[jnp.dot(jnp.concatenate([wu[b][h * cs:(h + 1) * cs, :DN_DK], qg[b][h]], axis=0).astype(BF16),
                   s_old[b][h].astype(BF16), preferred_element_type=F32) for h in heads] for b in range(nb)]
    v_new = [[wu[b][h * cs:(h + 1) * cs, DN_DK:] - wq[b][h][:cs] for h in heads] for b in range(nb)]
    o2 = [jnp.dot(bd(qk[b]), jnp.concatenate(v_new[b], axis=0).astype(BF16), preferred_element_type=F32)
          for b in range(nb)]
    for b in range(nb):
        for h in heads:
            o = wq[b][h][cs:] + o2[b][h * cs:(h + 1) * cs]
            s_scr[b, h] = s_old[b][h] * gl[b][h] + lax.dot_general(
                kd[b][h].astype(BF16), v_new[b][h].astype(BF16), (((0,), (0,)), ((), ())),
                preferred_element_type=F32)
            on = o * lax.rsqrt(jnp.mean(o * o, axis=-1, keepdims=True) + NORM_EPS) * ng_ref[...]
            dzh = dz[b][:, h * DN_DV:(h + 1) * DN_DV]
            o_ref[b, :, h * DN_DV:(h + 1) * DN_DV] = (on * _silu(dzh)).astype(BF16)

    @pl.when(c == n_c - 1)
    def _():
        nbuf_ref[...] = xp_scr[:, off:DN_HALO, :]
        sfin_ref[...] = s_scr[...]


def _deltanet(zmain3, zab3, buf, s0, cw, expa, dtb, ng, cs, nb):
    n_seq, seq_len, _ = zmain3.shape
    n_c = seq_len // cs
    cq, cz = Z_DNQKV // DN_CONV_CH, Z_DZ // DN_W
    row = pl.BlockSpec((1, LANES), lambda s, c: (0, 0))
    return pl.pallas_call(
        functools.partial(_dn_kernel, nb=nb, cs=cs, n_c=n_c),
        grid=(n_seq // nb, n_c),
        in_specs=[pl.BlockSpec((nb, cs, DN_CONV_CH), lambda s, c: (s, c, cq)),
                  pl.BlockSpec((nb, cs, DN_W), lambda s, c: (s, c, cz)),
                  pl.BlockSpec((nb, cs, LANES), lambda s, c: (s, c, 0)),
                  pl.BlockSpec((nb, DN_HALO, DN_CONV_CH), lambda s, c: (s, 0, 0)),
                  pl.BlockSpec((nb, DN_HEADS, DN_DK, DN_DV), lambda s, c: (s, 0, 0, 0)),
                  pl.BlockSpec((SHORT_CONV, DN_CONV_CH), lambda s, c: (0, 0)),
                  row, row, row],
        out_specs=[pl.BlockSpec((nb, cs, DN_W), lambda s, c: (s, c, 0)),
                   pl.BlockSpec((nb, SHORT_CONV - 1, DN_CONV_CH), lambda s, c: (s, 0, 0)),
                   pl.BlockSpec((nb, DN_HEADS, DN_DK, DN_DV), lambda s, c: (s, 0, 0, 0))],
        out_shape=[jax.ShapeDtypeStruct((n_seq, seq_len, DN_W), BF16),
                   jax.ShapeDtypeStruct((n_seq, SHORT_CONV - 1, DN_CONV_CH), F32),
                   jax.ShapeDtypeStruct((n_seq, DN_HEADS, DN_DK, DN_DV), F32)],
        scratch_shapes=[pltpu.VMEM((nb, 2 * DN_HALO, DN_CONV_CH), F32),
                        pltpu.VMEM((nb, DN_HEADS, DN_DK, DN_DV), F32)],
        compiler_params=_cparams(("parallel", "arbitrary")),
        name="deltanet",
    )(zmain3, zmain3, zab3, buf, s0, cw, expa, dtb, ng)


def _merge_body(branches, gates, x_ref, gm_ref, wb_ref, wo_ref, o_ref):
    acc = None
    for b, (br, gr) in enumerate(zip(branches, gates)):
        yb = jnp.dot(br[...], wb_ref[b], preferred_element_type=F32)
        t = _sigmoid(gr[...].astype(F32)) * yb
        acc = t if acc is None else acc + t
    out = jnp.dot(acc.astype(BF16), wo_ref[...], preferred_element_type=F32)
    o_ref[...] = x_ref[...] + gm_ref[0] * out


def _merge_kernel(att_ref, cv_ref, dn_ref, g0_ref, g1_ref, g2_ref, x_ref, gm_ref, wb_ref, wo_ref, o_ref):
    _merge_body((att_ref, cv_ref, dn_ref), (g0_ref, g1_ref, g2_ref), x_ref, gm_ref, wb_ref, wo_ref, o_ref)


def _merge(att, cv, dn, zmain, x, gate_m, wb, wo, tm, tps):
    m, d = x.shape
    br = pl.BlockSpec((tm, BRANCH_W), lambda i: (i, 0))
    return pl.pallas_call(
        _merge_kernel,
        grid=(m // tm,),
        in_specs=[br, br, br,
                  pl.BlockSpec((tm, d), lambda i: (i, 0)),
                  pl.BlockSpec((tm, d), lambda i: (i, 1)),
                  pl.BlockSpec((tm, d), lambda i: (i, 2)),
                  pl.BlockSpec((tm, d), lambda i: (i, 0)),
                  _mod_spec(gate_m, tm, tps),
                  pl.BlockSpec((N_BRANCH, BRANCH_W, d), lambda i: (0, 0, 0)),
                  pl.BlockSpec((d, d), lambda i: (0, 0))],
        out_specs=pl.BlockSpec((tm, d), lambda i: (i, 0)),
        out_shape=jax.ShapeDtypeStruct((m, d), F32),
        compiler_params=_cparams(("parallel",)),
        name="merge",
    )(att, cv, dn, zmain, zmain, zmain, x, gate_m, wb, wo)


def _merge_conv_kernel(att_ref, u1_ref, u2_ref, dn_ref, g0_ref, g1_ref, g2_ref, x_ref, gm_ref, wb_ref, wo_ref,
                       buf_ref, cw_ref, cb_ref, lg_ref, lb_ref, o_ref, nb_ref, xp_scr, sh_scr, cv_scr,
                       *, tps, sub):
    _conv_tile(pl.program_id(0) % tps == 0, u1_ref, u2_ref, buf_ref, cw_ref, cb_ref, lg_ref, lb_ref, cv_scr, nb_ref,
               xp_scr, sh_scr, cv_scr.shape[0], sub)
    _merge_body((att_ref, cv_scr, dn_ref), (g0_ref, g1_ref, g2_ref), x_ref, gm_ref, wb_ref, wo_ref, o_ref)


def _merge_conv(att, dn, zmain, x, gate_m, wb, wo, cbuf, cw, cb, lg, lb, tm, tps, sub):
    m, d = x.shape
    n_seq = m // (tm * tps)
    c1, c2 = Z_U1 // CONV_CH, Z_U2 // CONV_CH
    br = pl.BlockSpec((tm, BRANCH_W), lambda i: (i, 0))
    vec = pl.BlockSpec((1, CONV_CH), lambda i: (0, 0))
    return pl.pallas_call(
        functools.partial(_merge_conv_kernel, tps=tps, sub=sub),
        grid=(m // tm,),
        in_specs=[br,
                  pl.BlockSpec((tm, CONV_CH), lambda i: (i, c1)),
                  pl.BlockSpec((tm, CONV_CH), lambda i: (i, c2)),
                  br,
                  pl.BlockSpec((tm, d), lambda i: (i, 0)),
                  pl.BlockSpec((tm, d), lambda i: (i, 1)),
                  pl.BlockSpec((tm, d), lambda i: (i, 2)),
                  pl.BlockSpec((tm, d), lambda i: (i, 0)),
                  _mod_spec(gate_m, tm, tps),
                  pl.BlockSpec((N_BRANCH, BRANCH_W, d), lambda i: (0, 0, 0)),
                  pl.BlockSpec((d, d), lambda i: (0, 0)),
                  pl.BlockSpec((1, HALO, CONV_CH), lambda i: (i // tps, 0, 0)),
                  pl.BlockSpec((HALO, CONV_CH), lambda i: (0, 0)),
                  vec, vec, vec],
        out_specs=[pl.BlockSpec((tm, d), lambda i: (i, 0)),
                   pl.BlockSpec((1, CONV_WIDTH - 1, CONV_CH), lambda i: (i // tps, 0, 0))],
        out_shape=[jax.ShapeDtypeStruct((m, d), F32),
                   jax.ShapeDtypeStruct((n_seq, CONV_WIDTH - 1, CONV_CH), F32)],
        scratch_shapes=_conv_scratch(tm) + [pltpu.VMEM((tm, CONV_CH), BF16)],
        compiler_params=_cparams(("arbitrary",)),
        name="merge_conv",
    )(att, zmain, zmain, dn, zmain, zmain, zmain, x, gate_m, wb, wo, cbuf, cw, cb, lg, lb)


def _finish(x_ref, gf_ref, fg_ref, acc, final):
    xn = x_ref[...] + gf_ref[0] * acc
    if final:
        ms = jnp.mean(xn * xn, axis=-1, keepdims=True)
        xn = xn * lax.rsqrt(ms + NORM_EPS) * fg_ref[...]
    return xn


def _ffn_kernel(x_ref, g_ref, sc_ref, sh_ref, gf_ref, fg_ref, wa_ref, wb_ref, wo_ref, o_ref, *, final):
    h = _mod_norm(x_ref[...], g_ref[...], sc_ref[0], sh_ref[0]).astype(BF16)
    a = jnp.dot(h, wa_ref[...], preferred_element_type=F32)
    b = jnp.dot(h, wb_ref[...], preferred_element_type=F32)
    f = jnp.dot((_silu(a) * b).astype(BF16), wo_ref[...], preferred_element_type=F32)
    o_ref[...] = _finish(x_ref, gf_ref, fg_ref, f, final)


def _hidden_tile(f, limit):
    return max(t for t in range(LANES, min(f, limit) + 1, LANES) if f % t == 0)


def _ffn(x, g, scale, shift, gate_f, fg, w_i, w_o, tm, tps, final):
    m, d = x.shape
    f = w_o.shape[0]
    once = pl.Buffered(1)
    return pl.pallas_call(
        functools.partial(_ffn_kernel, final=final),
        grid=(m // tm,),
        in_specs=[pl.BlockSpec((tm, d), lambda i: (i, 0)),
                  pl.BlockSpec((1, d), lambda i: (0, 0)),
                  _mod_spec(scale, tm, tps), _mod_spec(shift, tm, tps), _mod_spec(gate_f, tm, tps),
                  pl.BlockSpec((1, d), lambda i: (0, 0)),
                  pl.BlockSpec((d, f), lambda i: (0, 0), pipeline_mode=once),
                  pl.BlockSpec((d, f), lambda i: (0, 1), pipeline_mode=once),
                  pl.BlockSpec((f, d), lambda i: (0, 0), pipeline_mode=once)],
        out_specs=pl.BlockSpec((tm, d), lambda i: (i, 0)),
        out_shape=jax.ShapeDtypeStruct((m, d), F32),
        compiler_params=_cparams(("parallel",)),
        name="ffn",
    )(x, g, scale, shift, gate_f, fg, w_i, w_i, w_o)


MOE_TB = 512
MOE_GRAN = 16
MOE_TM = 512
MOE_RUN_BITS = 6


def _moe_local_rows(n_e):
    return TOP_K * MOE_TB + n_e * MOE_GRAN


def _run_copies(n, src_ref, src0, dst_ref, dst0, sem, wait):
    off = 0
    for bit in reversed(range(MOE_RUN_BITS)):
        size = MOE_GRAN << bit
        take = (n & size) != 0
        cp = pltpu.make_async_copy(src_ref.at[pl.ds(pl.multiple_of(src0 + off, MOE_GRAN), size)],
                                   dst_ref.at[pl.ds(pl.multiple_of(dst0 + off, MOE_GRAN), size)], sem)

        @pl.when(take)
        def _():
            if wait:
                cp.wait()
            else:
                cp.start()

        off = off + jnp.where(take, size, 0)


def _route_kernel(start_ref, x_ref, g_ref, sc_ref, sh_ref, wr_ref, *rest, n_e, n_steps, nk, aliased):
    if aliased:
        rest = rest[1:]
    route_ref, meta_ref, fill_ref, rows_ref, sorted_scr, cur_scr, sem = rest
    step = pl.program_id(0)
    tb = x_ref.shape[0] // nk
    lr = sorted_scr.shape[1]
    blocks = range(nk)

    @pl.when(step == 0)
    def _():
        for e in range(n_e):
            cur_scr[e] = start_ref[e]

    def rows_of(ref0, k):
        return ref0 if ref0.shape[0] == 1 else ref0[k * tb:(k + 1) * tb]

    sc, sh = sc_ref[0], sh_ref[0]
    h = [_mod_norm(x_ref[k * tb:(k + 1) * tb, :], g_ref[...], rows_of(sc, k), rows_of(sh, k)) for k in blocks]
    w_hi, w_mid, _ = _split3(wr_ref[...])
    logits = []
    for k in blocks:
        h_hi, h_mid, _ = _split3(h[k])
        logits.append(jnp.dot(h_hi, w_hi, preferred_element_type=F32)
                      + (jnp.dot(h_hi, w_mid, preferred_element_type=F32)
                         + jnp.dot(h_mid, w_hi, preferred_element_type=F32)))
    lane = lax.broadcasted_iota(jnp.int32, (tb, LANES), 1)
    ti = lax.broadcasted_iota(jnp.int32, (tb, tb), 0)
    tj = lax.broadcasted_iota(jnp.int32, (tb, tb), 1)
    earlier = jnp.where(ti > tj, 1.0, 0.0).astype(BF16)
    ei = lax.broadcasted_iota(jnp.int32, (LANES, LANES), 0)
    ej = lax.broadcasted_iota(jnp.int32, (LANES, LANES), 1)
    before = jnp.where(ei < ej, 1.0, 0.0).astype(BF16)
    rr = lax.broadcasted_iota(jnp.int32, (tb, lr), 1)
    i1, i2, w1, w2, sel = [], [], [], [], []
    for k in blocks:
        lg = jnp.where(lane < n_e, logits[k], NEG_BIG)
        m1 = lg.max(axis=-1, keepdims=True)
        a1 = jnp.where(lg == m1, lane, LANES).min(axis=-1, keepdims=True)
        lg2 = jnp.where(lane == a1, NEG_BIG, lg)
        m2 = lg2.max(axis=-1, keepdims=True)
        a2 = jnp.where(lg2 == m2, lane, LANES).min(axis=-1, keepdims=True)
        e2 = jnp.exp(m2 - m1)
        i1.append(a1)
        i2.append(a2)
        w1.append(1.0 / (1.0 + e2))
        w2.append(e2 / (1.0 + e2))
        sel.append(jnp.where(lane == a1, 1.0, 0.0) + jnp.where(lane == a2, 1.0, 0.0))
    rank = [jnp.dot(earlier, sel[k].astype(BF16), preferred_element_type=F32) for k in blocks]
    cpad = [jnp.floor((jnp.sum(sel[k], axis=0, keepdims=True) + (MOE_GRAN - 1)) * (1.0 / MOE_GRAN)) * MOE_GRAN
            for k in blocks]
    loc = [jnp.dot(jnp.broadcast_to(cpad[k], (SUBLANES, LANES)).astype(BF16), before,
                   preferred_element_type=F32)[0:1] for k in blocks]
    for k in blocks:
        pos = loc[k] + rank[k]
        p1 = jnp.sum(jnp.where(lane == i1[k], pos, 0.0), axis=-1, keepdims=True)
        p2 = jnp.sum(jnp.where(lane == i2[k], pos, 0.0), axis=-1, keepdims=True)
        route_ref[k * tb:(k + 1) * tb, :] = jnp.where(
            lane == 0, p1, jnp.where(lane == 1, p2, jnp.where(lane == 2, w1[k], jnp.where(lane == 3, w2[k], 0.0))))
        perm = jnp.where((rr == p1.astype(jnp.int32)) | (rr == p2.astype(jnp.int32)), 1.0, 0.0).astype(BF16)
        sorted_scr[k] = lax.dot_general(perm, h[k].astype(BF16), (((0,), (0,)), ((), ())),
                                        preferred_element_type=F32).astype(BF16)
    runs = []
    for k in blocks:
        cp_i = cpad[k].astype(jnp.int32)
        loc_i = loc[k].astype(jnp.int32)
        blk = step * nk + k
        for e in range(n_e):
            n = cp_i[0, e]
            dst0 = cur_scr[e]
            meta_ref[blk, e] = dst0
            meta_ref[blk, n_e + e] = n
            cur_scr[e] = dst0 + n
            runs.append((k, e, n, loc_i[0, e], dst0))
    for wait in (False, True):
        for k, e, n, src0, dst0 in runs:
            _run_copies(n, sorted_scr.at[k], src0, rows_ref.at[e], dst0, sem, wait)

    @pl.when(step == n_steps - 1)
    def _():
        for e in range(n_e):
            fill_ref[e] = cur_scr[e]


def _route(x, g, scale, shift, w_r, start, rows_in, n_e, cap, tps):
    m, d = x.shape
    tb = MOE_TB
    n_b = m // tb
    nk = 2 if (n_b % 2 == 0 and tps % 2 == 0) else 1
    lr = _moe_local_rows(n_e)
    aliased = rows_in is not None
    smem = pl.BlockSpec(memory_space=pltpu.SMEM)
    hbm = pl.BlockSpec(memory_space=pl.ANY)
    in_specs = [smem,
                pl.BlockSpec((nk * tb, d), lambda i: (i, 0)),
                pl.BlockSpec((1, d), lambda i: (0, 0)),
                _mod_spec(scale, nk * tb, tps // nk), _mod_spec(shift, nk * tb, tps // nk),
                pl.BlockSpec((d, LANES), lambda i: (0, 0))]
    args = [start, x, g, scale, shift, w_r]
    if aliased:
        in_specs.append(hbm)
        args.append(rows_in)
    return pl.pallas_call(
        functools.partial(_route_kernel, n_e=n_e, n_steps=n_b // nk, nk=nk, aliased=aliased),
        grid=(n_b // nk,),
        in_specs=in_specs,
        out_specs=[pl.BlockSpec((nk * tb, LANES), lambda i: (i, 0)), smem, smem, hbm],
        out_shape=[jax.ShapeDtypeStruct((m, LANES), F32),
                   jax.ShapeDtypeStruct((n_b, 2 * n_e), jnp.int32),
                   jax.ShapeDtypeStruct((n_e,), jnp.int32),
                   jax.ShapeDtypeStruct((n_e, cap, d), BF16)],
        scratch_shapes=[pltpu.VMEM((nk, lr, d), BF16), pltpu.SMEM((n_e,), jnp.int32), pltpu.SemaphoreType.DMA(())],
        input_output_aliases={len(args) - 1: 3} if aliased else {},
        compiler_params=_cparams(("arbitrary",)),
        name="moe_route",
    )(*args)


def _experts_kernel(te_ref, tr_ref, nv_ref, x_ref, wa_ref, wb_ref, wo_ref, y_ref, xs_scr, acc_scr, *, n_f):
    t = pl.program_id(0)
    j = pl.program_id(1)
    nv = nv_ref[t]

    @pl.when(nv > 0)
    def _():
        @pl.when(j == 0)
        def _():
            row = lax.broadcasted_iota(jnp.int32, xs_scr.shape, 0)
            xs_scr[...] = jnp.where(row < nv, x_ref[0].astype(F32), 0.0).astype(BF16)
            acc_scr[...] = jnp.zeros_like(acc_scr)

        xs = xs_scr[...]
        a = jnp.dot(xs, wa_ref[0], preferred_element_type=F32)
        b = jnp.dot(xs, wb_ref[0], preferred_element_type=F32)
        acc_scr[...] += jnp.dot((_silu(a) * b).astype(BF16), wo_ref[0], preferred_element_type=F32)

        @pl.when(j == n_f - 1)
        def _():
            y_ref[0] = acc_scr[...].astype(BF16)

    @pl.when((nv == 0) & (j == n_f - 1))
    def _():
        y_ref[0] = jnp.zeros(y_ref.shape[1:], BF16)


def _experts(rows, w_i, w_o, tile_e, tile_r, tile_nv):
    n_e, cap, d = rows.shape
    f = w_o.shape[1]
    tf = _hidden_tile(f, 1792)
    n_f = f // tf
    tm = MOE_TM
    n_t = tile_e.shape[0]

    def jv(t, j, nv):
        return jnp.where(nv[t] > 0, j, 0)

    return pl.pallas_call(
        functools.partial(_experts_kernel, n_f=n_f),
        grid_spec=pltpu.PrefetchScalarGridSpec(
            num_scalar_prefetch=3,
            grid=(n_t, n_f),
            in_specs=[pl.BlockSpec((1, tm, d), lambda t, j, te, tr, nv: (te[t], tr[t], 0)),
                      pl.BlockSpec((1, d, tf), lambda t, j, te, tr, nv: (te[t], 0, jv(t, j, nv))),
                      pl.BlockSpec((1, d, tf), lambda t, j, te, tr, nv: (te[t], 0, jv(t, j, nv) + n_f)),
                      pl.BlockSpec((1, tf, d), lambda t, j, te, tr, nv: (te[t], jv(t, j, nv), 0))],
            out_specs=pl.BlockSpec((1, tm, d), lambda t, j, te, tr, nv: (te[t], tr[t], 0)),
            scratch_shapes=[pltpu.VMEM((tm, d), BF16), pltpu.VMEM((tm, d), F32)]),
        out_shape=jax.ShapeDtypeStruct((n_e, cap, d), BF16),
        compiler_params=_cparams(("arbitrary", "arbitrary")),
        name="moe_experts",
    )(tile_e, tile_r, tile_nv, rows, w_i, w_i, w_o)


def _combine_kernel(meta_ref, y_ref, route_ref, x_ref, gf_ref, fg_ref, o_ref, yl_scr, sem, *, n_e, n_b, final):
    b = pl.program_id(0)
    tb = x_ref.shape[0]
    lr = yl_scr.shape[1]
    slot = b % 2

    def fetch(blk, slt, wait):
        loc = 0
        for e in range(n_e):
            n = meta_ref[blk, n_e + e]
            _run_copies(n, y_ref.at[e], meta_ref[blk, e], yl_scr.at[slt], loc, sem.at[slt], wait)
            loc = loc + n
        return loc

    @pl.when(b == 0)
    def _():
        fetch(0, 0, False)

    @pl.when(b + 1 < n_b)
    def _():
        fetch(b + 1, 1 - slot, False)

    loc = fetch(b, slot, True)
    row = lax.broadcasted_iota(jnp.int32, (lr, yl_scr.shape[2]), 0)
    yl = jnp.where(row < loc, yl_scr[slot].astype(F32), 0.0).astype(BF16)
    route = route_ref[...]
    p1 = route[:, 0:1].astype(jnp.int32)
    p2 = route[:, 1:2].astype(jnp.int32)
    rr = lax.broadcasted_iota(jnp.int32, (tb, lr), 1)
    pw = jnp.where(rr == p1, route[:, 2:3], 0.0) + jnp.where(rr == p2, route[:, 3:4], 0.0)
    mix = jnp.dot(pw.astype(BF16), yl, preferred_element_type=F32)
    o_ref[...] = _finish(x_ref, gf_ref, fg_ref, mix, final)


def _combine(meta, y, route, x, gate_f, fg, n_e, tps, final):
    m, d = x.shape
    tb = MOE_TB
    lr = _moe_local_rows(n_e)
    return pl.pallas_call(
        functools.partial(_combine_kernel, n_e=n_e, n_b=m // tb, final=final),
        grid_spec=pltpu.PrefetchScalarGridSpec(
            num_scalar_prefetch=1,
            grid=(m // tb,),
            in_specs=[pl.BlockSpec(memory_space=pl.ANY),
                      pl.BlockSpec((tb, LANES), lambda i, mt: (i, 0)),
                      pl.BlockSpec((tb, d), lambda i, mt: (i, 0)),
                      _mod_spec(gate_f, tb, tps),
                      pl.BlockSpec((1, d), lambda i, mt: (0, 0))],
            out_specs=pl.BlockSpec((tb, d), lambda i, mt: (i, 0)),
            scratch_shapes=[pltpu.VMEM((2, lr, d), BF16), pltpu.SemaphoreType.DMA((2,))]),
        out_shape=jax.ShapeDtypeStruct((m, d), F32),
        compiler_params=_cparams(("arbitrary",)),
        name="moe_combine",
    )(meta, y, route, x, gate_f, fg)


def _moe_sparse(streams, g, fg, w_r, w_i, w_o, final):
    n_e = w_o.shape[0]
    d = streams[0]["x"].shape[1]
    m_all = sum(s["x"].shape[0] for s in streams)
    n_b_all = m_all // MOE_TB
    worst_e = m_all + n_b_all * (MOE_GRAN - 1)
    worst = TOP_K * m_all + n_b_all * n_e * (MOE_GRAN - 1)
    cap = (-(-worst_e // MOE_TM) + 1) * MOE_TM
    n_t = -(-worst // MOE_TM) + n_e
    start = jnp.zeros((n_e,), jnp.int32)
    rows = None
    routed = []
    for s in streams:
        route, meta, start, rows = _route(s["x"], g, s["scale"], s["shift"], w_r, start, rows, n_e, cap, s["tps"])
        routed.append((route, meta))
    fill = start
    tiles_e = (fill + MOE_TM - 1) // MOE_TM
    first = jnp.cumsum(tiles_e) - tiles_e
    t = jnp.arange(n_t, dtype=jnp.int32)
    e_of = jnp.sum((t[:, None] >= (first + tiles_e)[None, :]).astype(jnp.int32), axis=1)
    valid = e_of < n_e
    e_cl = jnp.minimum(e_of, n_e - 1)
    r_of = t - first[e_cl]
    nv = jnp.where(valid, jnp.clip(fill[e_cl] - r_of * MOE_TM, 0, MOE_TM), 0).astype(jnp.int32)
    tile_e = jnp.where(valid, e_cl, n_e - 1).astype(jnp.int32)
    tile_r = jnp.where(valid, r_of, cap // MOE_TM - 1).astype(jnp.int32)
    y = _experts(rows, w_i, w_o, tile_e, tile_r, nv)
    return [_combine(meta, y, route, s["x"], s["gate_f"], fg, n_e, s["tps"], final)
            for s, (route, meta) in zip(streams, routed)]


def _prep_weights(w_in, w_branch, w_out, ffn_w_in, ffn_w_out, router_w, moe_w_in, moe_w_out):
    depth = w_in.shape[0]
    w_main = jnp.concatenate([w_in[..., OFF_GATE:], w_in[..., OFF_DN_QKV:OFF_DN_Z], w_in[..., OFF_AQ:OFF_GLU],
                              w_in[..., OFF_GLU:OFF_DN_QKV], w_in[..., OFF_DN_Z:OFF_DN_A]], axis=-1).astype(BF16)
    w_ab = jnp.pad(w_in[..., OFF_DN_A:OFF_GATE], ((0, 0), (0, 0), (0, LANES - 2 * DN_HEADS))).astype(BF16)
    w_r = jnp.pad(router_w, ((0, 0), (0, 0), (0, LANES - router_w.shape[-1])))
    return dict(w_main=w_main, w_ab=w_ab, w_branch=w_branch.astype(BF16), w_out=w_out.astype(BF16),
                ffn_w_in=ffn_w_in.astype(BF16), ffn_w_out=ffn_w_out.astype(BF16), w_r=w_r,
                moe_w_in=moe_w_in.astype(BF16), moe_w_out=moe_w_out.astype(BF16), depth=depth)


class _Stream:
    def __init__(self, x3, mod, caches):
        self.n_seq, self.seq_len, self.d = x3.shape
        self.m = self.n_seq * self.seq_len
        self.x = x3.reshape(self.m, self.d)
        self.mod = mod
        self.caches = caches
        self.decode = caches is not None
        self.states = ([], [], [], [], [])

    def tile(self, want):
        return self.m if self.decode else want

    def tps(self, t):
        return 1 if self.decode else self.seq_len // t

    def modv(self, l, idx, t):
        v = self.mod[l][:, idx]
        if self.decode:
            return jnp.repeat(v, self.seq_len, axis=0).reshape(self.m // t, t, self.d)
        return v.reshape(self.n_seq, 1, self.d)


def _mixers(s, l, pw, p):
    n_seq, seq_len, m, decode, caches = s.n_seq, s.seq_len, s.m, s.decode, s.caches
    x = s.x
    tm = s.tile(512)
    zmain, zab = _in_proj(x, p["norm_g"][l, 0:1], s.modv(l, 1, tm), s.modv(l, 0, tm),
                          pw["w_main"][l], pw["w_ab"][l], tm, s.tps(tm))
    if decode:
        cache_k, cache_v = caches[0][l], caches[1][l]
        nr = cache_k.shape[1]
        tab = _bias_table(p["rel_bias"][l], PAST_LEN, seq_len, PAST_LEN - nr, nr + seq_len)
        att = _attn_decode(zmain, cache_k.reshape(n_seq, nr, ATT_W), cache_v.reshape(n_seq, nr, ATT_W),
                           tab[:, :, :nr], tab[:, :, nr:], n_seq, seq_len)
        keep = seq_len
    else:
        qb, tq = 2 * CHUNK, N_BACK * CHUNK
        tab = _bias_table(p["rel_bias"][l], tq, qb, 0, tq + qb)
        att = _attn_prompt(zmain, tab, n_seq, seq_len)
        keep = min(N_BACK * CHUNK, seq_len)
    z3 = zmain.reshape(n_seq, seq_len, Z_W)
    k_rows = z3[:, seq_len - keep:, Z_K:Z_K + ATT_W].astype(F32).reshape(n_seq, keep, ATT_HEADS, ATT_HEAD_DIM)
    v_rows = z3[:, seq_len - keep:, Z_V:Z_V + ATT_W].astype(F32).reshape(n_seq, keep, ATT_HEADS, ATT_HEAD_DIM)
    if decode:
        cbuf = jnp.pad(caches[2][l], ((0, 0), (HALO - (CONV_WIDTH - 1), 0), (0, 0)))
        t_rows, sub = seq_len, seq_len
    else:
        cbuf = jnp.zeros((n_seq, HALO, CONV_CH), F32)
        t_rows, sub = 512, 64
    cw = jnp.pad(p["conv_w"][l], ((0, HALO - CONV_WIDTH), (0, 0)))
    conv_args = (cbuf, cw, p["conv_b"][l][None], p["conv_ln_g"][l][None], p["conv_ln_b"][l][None])
    if decode:
        cv, new_cbuf = _conv_module(zmain, *conv_args, n_seq, seq_len, t_rows, sub)
    if decode:
        dbuf = jnp.pad(caches[3][l], ((0, 0), (DN_HALO - (SHORT_CONV - 1), 0), (0, 0)))
        s0 = caches[4][l]
    else:
        dbuf = jnp.zeros((n_seq, DN_HALO, DN_CONV_CH), F32)
        s0 = jnp.zeros((n_seq, DN_HEADS, DN_DK, DN_DV), F32)
    cs = seq_len if seq_len <= CHUNK else CHUNK
    lane_pad = (0, LANES - DN_HEADS)
    expa = jnp.pad(jnp.exp(p["dn_A_log"][l]), lane_pad)[None]
    dtb = jnp.pad(p["dn_dt_bias"][l], lane_pad)[None]
    dn, new_dbuf, s_new = _deltanet(z3, zab.reshape(n_seq, seq_len, LANES), dbuf, s0, p["dn_conv_w"][l],
                                    expa, dtb, p["dn_norm_g"][l][None], cs, 4)
    if decode:
        s.x = _merge(att, cv, dn.reshape(m, DN_W), zmain, x, s.modv(l, 2, tm), pw["w_branch"][l], pw["w_out"][l],
                     tm, s.tps(tm))
    else:
        s.x, new_cbuf = _merge_conv(att, dn.reshape(m, DN_W), zmain, x, s.modv(l, 2, tm), pw["w_branch"][l],
                                    pw["w_out"][l], *conv_args, tm, s.tps(tm), sub)
    for lst, st in zip(s.states, (k_rows, v_rows, new_cbuf, new_dbuf, s_new)):
        lst.append(st)


def _channel_mixer(streams, l, pw, p, final):
    g = p["norm_g"][l, 1:2]
    fg = p["final_norm_g"][None]
    if l % 2 == 0:
        for s in streams:
            t = s.tile(512)
            s.x = _ffn(s.x, g, s.modv(l, 4, t), s.modv(l, 3, t), s.modv(l, 5, t), fg, pw["ffn_w_in"][l // 2],
                       pw["ffn_w_out"][l // 2], t, s.tps(t), final)
    else:
        t = MOE_TB
        parts = [dict(x=s.x, scale=s.modv(l, 4, t), shift=s.modv(l, 3, t), gate_f=s.modv(l, 5, t),
                      tps=s.tps(t)) for s in streams]
        for s, xn in zip(streams, _moe_sparse(parts, g, fg, pw["w_r"][l // 2], pw["moe_w_in"][l // 2],
                                              pw["moe_w_out"][l // 2], final)):
            s.x = xn


def kernel(x_prompt, x_sample, c_prompt, c_sample, cache_attn_k, cache_attn_v, state_conv, state_dn_conv, state_dn,
           w_ada, b_ada, norm_g, w_in, rel_bias, conv_w, conv_b, conv_ln_g, conv_ln_b, dn_conv_w, dn_A_log,
           dn_dt_bias, dn_norm_g, w_branch, w_out, ffn_w_in, ffn_w_out, router_w, moe_w_in, moe_w_out,
           final_norm_g):
    depth = w_in.shape[0]
    d = x_prompt.shape[-1]
    n_p = c_prompt.shape[0]
    p = dict(norm_g=norm_g, rel_bias=rel_bias, conv_w=conv_w, conv_b=conv_b, conv_ln_g=conv_ln_g,
             conv_ln_b=conv_ln_b, dn_conv_w=dn_conv_w, dn_A_log=dn_A_log, dn_dt_bias=dn_dt_bias,
             dn_norm_g=dn_norm_g, final_norm_g=final_norm_g)
    pw = _prep_weights(w_in, w_branch, w_out, ffn_w_in, ffn_w_out, router_w, moe_w_in, moe_w_out)
    mod = _ada(jnp.concatenate([c_prompt, c_sample], axis=0), w_ada, b_ada)
    mod = mod.reshape(depth, -1, 6, d)
    streams = [_Stream(x_prompt, mod[:, :n_p], None),
               _Stream(x_sample, mod[:, n_p:], (cache_attn_k, cache_attn_v, state_conv, state_dn_conv, state_dn))]
    for l in range(depth):
        for s in streams:
            _mixers(s, l, pw, p)
        _channel_mixer(streams, l, pw, p, final=l == depth - 1)
    ys = tuple(s.x.reshape(s.n_seq, s.seq_len, d) for s in streams)
    return ys + tuple(jnp.stack(st) for s in streams for st in s.states)
```
